```python
import math
import jax, jax.numpy as jnp
from jax import lax
import numpy as np

D_MODEL = 4096
BATCH = 16
SEQ = 256
DEPTH = 2
DEC_BATCH = 8
DEC_SEQ = 2048
PAST_LEN = 256

GRID_W = 64
N_AB = (DEPTH + 1) // 2
N_C = DEPTH // 2
A_WIDTH = D_MODEL // 2
A_HEADS = 8
A_V = A_WIDTH // A_HEADS
A_QK = A_V // 2
B_WIDTH = D_MODEL // 2
B_KEY = 128
B_HEADS = B_WIDTH // B_KEY
B_VAL = B_WIDTH // B_HEADS
C_HEADS = 8
C_KEY = D_MODEL // 2 // C_HEADS
C_VAL = D_MODEL // C_HEADS
C_RANK = 16
C_GATE_NORM = 16.0
N_EXPERTS = 16
EC_FACTOR = 2
D_EXPERT = D_MODEL // 2
CHUNK = 64
Q_BLOCK = 128
ROPE_BASE = 10000.0
EPS = 1e-6
F32 = jnp.float32

kernel_name = "hybrid_diffusion_diffattn_hgrn2_gla_ecmoe_step"


def rmsnorm(x, g):
    x32 = x.astype(F32)
    y = x32 * lax.rsqrt(jnp.mean(x32 * x32, axis=-1, keepdims=True) + EPS)
    return (y * g.astype(F32)).astype(x.dtype)


def adaln(cond, w, b):
    m = jax.nn.silu(cond) @ w + b
    return [a[:, None, :] for a in jnp.split(m, 6, axis=-1)]


def axial_rope_tables(n_tokens):
    rows = n_tokens // GRID_W
    row = jnp.repeat(jnp.arange(rows, dtype=F32), GRID_W)
    col = jnp.tile(jnp.arange(GRID_W, dtype=F32), rows)
    axis_dim = A_QK // 2
    inv = ROPE_BASE ** (-jnp.arange(0, axis_dim, 2, dtype=F32) / axis_dim)
    ar = row[:, None] * inv
    ac = col[:, None] * inv
    return (jnp.cos(ar), jnp.sin(ar)), (jnp.cos(ac), jnp.sin(ac))


def _rotate(x, cs):
    cos = cs[0][:, None, None, :].astype(x.dtype)
    sin = cs[1][:, None, None, :].astype(x.dtype)
    x1, x2 = jnp.split(x, 2, axis=-1)
    return jnp.concatenate([x1 * cos - x2 * sin, x1 * sin + x2 * cos], axis=-1)


def axial_rope(x, rope):
    xr, xc = jnp.split(x, 2, axis=-1)
    return jnp.concatenate([_rotate(xr, rope[0]), _rotate(xc, rope[1])], axis=-1)


def diff_attention(q, k, v, lam):
    bsz, tq, nh = q.shape[:3]
    nb = tq // Q_BLOCK
    qb = q.reshape(bsz, nb, Q_BLOCK, nh, 2, A_QK).transpose(1, 0, 2, 3, 4, 5)
    scale = A_QK ** -0.5

    def block(qi):
        s = jnp.einsum('bqhmd,bkhmd->bhmqk', qi, k).astype(F32) * scale
        p = jax.nn.softmax(s, axis=-1)
        w = (p[:, :, 0] - lam * p[:, :, 1]).astype(v.dtype)
        return jnp.einsum('bhqk,bkhd->bqhd', w, v)

    o = lax.map(block, qb)
    return o.transpose(1, 0, 2, 3, 4).reshape(bsz, tq, nh, A_V)


def chunk_gated_linear(q, k, v, log_f, s0):
    bsz, nh, t, dk = q.shape
    dv = v.shape[-1]
    n = t // CHUNK

    def to_chunks(a):
        return a.reshape(bsz, nh, n, CHUNK, a.shape[-1]).transpose(2, 0, 1, 3, 4)

    qc, kc, vc, gc = to_chunks(q), to_chunks(k), to_chunks(v), to_chunks(log_f)
    mask = jnp.tril(jnp.ones((CHUNK, CHUNK), bool))[:, :, None]

    def step(s, inp):
        qi, ki, vi, gi = inp
        b = jnp.cumsum(gi.astype(F32), axis=2)
        b_last = b[:, :, -1:]
        o_inter = jnp.einsum('bhcd,bhde->bhce', qi.astype(F32) * jnp.exp(b), s)
        diff = b[:, :, :, None, :] - b[:, :, None, :, :]
        decay = jnp.exp(jnp.where(mask, diff, -jnp.inf))
        a = jnp.einsum('bhid,bhjd,bhijd->bhij', qi.astype(F32), ki.astype(F32), decay)
        o = o_inter + jnp.einsum('bhij,bhje->bhie', a, vi.astype(F32))
        k_dec = ki.astype(F32) * jnp.exp(b_last - b)
        s_new = jnp.exp(b_last)[:, :, 0, :, None] * s + jnp.einsum('bhcd,bhce->bhde', k_dec, vi.astype(F32))
        return s_new, o

    s_fin, o = lax.scan(step, s0.astype(F32), (qc, kc, vc, gc))
    o = o.transpose(1, 2, 0, 3, 4).reshape(bsz, nh, t, dv)
    return o.astype(v.dtype), s_fin


def bidir_scan(q, k_f, k_b, v, g_f, g_b, s0_f, s0_b):
    o_f, s_f = chunk_gated_linear(q, k_f, v, g_f, s0_f)
    fl = lambda a: jnp.flip(a, axis=2)
    o_b, s_b = chunk_gated_linear(fl(q), fl(k_b), fl(v), fl(g_b), s0_b)
    return o_f + fl(o_b), s_f, s_b


def _heads(a, d):
    bsz, t = a.shape[:2]
    return a.reshape(bsz, t, -1, d).transpose(0, 2, 1, 3)


def ab_mixer(h, w_in, lam_p, subln, lb, onorm, w_out, lam_init, rope=None, ctx=None):
    bsz, t, _ = h.shape
    sizes = [A_WIDTH] * 3 + [B_WIDTH] * 5
    cuts = np.cumsum(sizes)[:-1].tolist()
    q_a, k_a, v_a, q_b, z_f, z_b, i_b, og_b = jnp.split(h @ w_in, cuts, axis=-1)
    q_a = q_a.reshape(bsz, t, A_HEADS, 2, A_QK)
    k_a = k_a.reshape(bsz, t, A_HEADS, 2, A_QK)
    v_a = v_a.reshape(bsz, t, A_HEADS, A_V)
    lam_p = lam_p.astype(F32)
    lam = jnp.exp(jnp.sum(lam_p[0] * lam_p[1])) - jnp.exp(jnp.sum(lam_p[2] * lam_p[3])) + lam_init
    if ctx is None:
        k_all, v_all = k_a, v_a
    else:
        ck, cv, cs = ctx
        q_a = axial_rope(q_a, rope)
        k_all = jnp.concatenate([axial_rope(k_a, rope), ck.reshape(bsz, -1, A_HEADS, 2, A_QK)], axis=1)
        v_all = jnp.concatenate([v_a, cv], axis=1)
    o_a = diff_attention(q_a, k_all, v_all, lam)
    o_a = (rmsnorm(o_a, subln) * (1.0 - lam_init)).reshape(bsz, t, A_WIDTH)
    lb = lb.astype(F32)
    f_f = lb[0] + (1.0 - lb[0]) * jax.nn.sigmoid(z_f.astype(F32))
    f_b = lb[1] + (1.0 - lb[1]) * jax.nn.sigmoid(z_b.astype(F32))
    if ctx is None:
        s0_f = jnp.zeros((bsz, B_HEADS, B_KEY, B_VAL), F32)
        s0_b = s0_f
    else:
        s0_f, s0_b = cs[:, 0], cs[:, 1]
    o_b, s_f, s_b = bidir_scan(_heads(q_b, B_KEY), _heads((1.0 - f_f).astype(h.dtype), B_KEY),
                               _heads((1.0 - f_b).astype(h.dtype), B_KEY), _heads(i_b, B_VAL),
                               _heads(jnp.log(f_f), B_KEY), _heads(jnp.log(f_b), B_KEY), s0_f, s0_b)
    o_b = rmsnorm(o_b.transpose(0, 2, 1, 3), onorm) * jax.nn.silu(og_b.reshape(bsz, t, B_HEADS, B_VAL))
    out = jnp.concatenate([o_a, o_b.reshape(bsz, t, B_WIDTH)], axis=-1) @ w_out
    if ctx is None:
        states = jnp.stack([s_f, s_b], axis=1).astype(h.dtype)
        return out, (k_a.reshape(bsz, t, A_HEADS, 2 * A_QK), v_a, states)
    return out


def gla_mixer(h, w_in, gate_w2, gate_b, onorm, w_out, ctx_state=None):
    bsz, t, _ = h.shape
    sizes = [C_HEADS * C_KEY, C_HEADS * C_KEY, C_HEADS * C_VAL, C_HEADS * C_VAL, C_RANK, C_RANK]
    cuts = np.cumsum(sizes)[:-1].tolist()
    q, k, v, r, lr_f, lr_b = jnp.split(h @ w_in, cuts, axis=-1)
    g_f = jax.nn.log_sigmoid((lr_f @ gate_w2[0] + gate_b[0]).astype(F32)) / C_GATE_NORM
    g_b = jax.nn.log_sigmoid((lr_b @ gate_w2[1] + gate_b[1]).astype(F32)) / C_GATE_NORM
    if ctx_state is None:
        s0_f = jnp.zeros((bsz, C_HEADS, C_KEY, C_VAL), F32)
        s0_b = s0_f
    else:
        s0_f, s0_b = ctx_state[:, 0], ctx_state[:, 1]
    kh = _heads(k, C_KEY)
    o, s_f, s_b = bidir_scan(_heads(q * (C_KEY ** -0.5), C_KEY), kh, kh, _heads(v, C_VAL),
                             _heads(g_f, C_KEY), _heads(g_b, C_KEY), s0_f, s0_b)
    o = rmsnorm(o.transpose(0, 2, 1, 3), onorm) * jax.nn.silu(r.reshape(bsz, t, C_HEADS, C_VAL))
    out = o.reshape(bsz, t, C_HEADS * C_VAL) @ w_out
    if ctx_state is None:
        return out, jnp.stack([s_f, s_b], axis=1).astype(h.dtype)
    return out


def ec_moe(h, router_w, w_gate, w_up, w_down):
    bsz, t, d = h.shape
    flat = h.reshape(-1, d)
    cap = EC_FACTOR * flat.shape[0] // N_EXPERTS
    aff = jax.nn.softmax((flat @ router_w).astype(F32), axis=-1)
    gate, idx = lax.top_k(aff.T, cap)
    xe = flat[idx]
    hid = jax.nn.silu(jnp.einsum('ecd,edf->ecf', xe, w_gate)) * jnp.einsum('ecd,edf->ecf', xe, w_up)
    ye = jnp.einsum('ecf,efd->ecd', hid, w_down) * gate[..., None].astype(h.dtype)
    out = jnp.zeros_like(flat).at[idx.reshape(-1)].add(ye.reshape(-1, d))
    return out.reshape(bsz, t, d)


def setup_inputs(seed: int = 0) -> dict:
    key = jax.random.key(seed)
    ks = iter(jax.random.split(key, 32))

    def nrm(shape, s):
        return jax.random.normal(next(ks), shape, F32) * s

    def gain(shape):
        return 1.0 + nrm(shape, 0.02)

    d = D_MODEL
    c_in = C_HEADS * (2 * C_KEY + 2 * C_VAL) + 2 * C_RANK
    return {
        "x_prompt": nrm((BATCH, SEQ, d), 1.0),
        "x_sample": nrm((DEC_BATCH, DEC_SEQ, d), 1.0),
        "cache_k_a": nrm((DEC_BATCH, N_AB, PAST_LEN, A_HEADS, 2 * A_QK), 1.0),
        "cache_v_a": nrm((DEC_BATCH, N_AB, PAST_LEN, A_HEADS, A_V), 1.0),
        "state_hgrn": nrm((DEC_BATCH, N_AB, 2, B_HEADS, B_KEY, B_VAL), 0.5),
        "state_gla": nrm((DEC_BATCH, N_C, 2, C_HEADS, C_KEY, C_VAL), 1.0),
        "c": nrm((DEC_BATCH, d), 1.0),
        "c_ctx": nrm((d,), 1.0),
        "ada_w": nrm((DEPTH, d, 6 * d), 0.5 * d ** -0.5),
        "ada_b": nrm((DEPTH, 6 * d), 0.02),
        "norm_mix": gain((DEPTH, d)),
        "norm_ffn": gain((DEPTH, d)),
        "ab_w_in": nrm((N_AB, d, 3 * A_WIDTH + 5 * B_WIDTH), d ** -0.5),
        "ab_lambda": nrm((N_AB, 4, A_QK), 0.1),
        "ab_subln": gain((N_AB, A_V)),
        "ab_lb_logits": nrm((DEPTH + 1, 2, B_WIDTH), 0.5),
        "ab_onorm": gain((N_AB, B_VAL)),
        "ab_w_out": nrm((N_AB, d, d), d ** -0.5),
        "c_w_in": nrm((N_C, d, c_in), d ** -0.5),
        "c_gate_w2": nrm((N_C, 2, C_RANK, C_HEADS * C_KEY), C_RANK ** -0.5),
        "c_gate_b": nrm((N_C, 2, C_HEADS * C_KEY), 0.1),
        "c_onorm": gain((N_C, C_VAL)),
        "c_w_out": nrm((N_C, d, d), d ** -0.5),
        "router_w": nrm((DEPTH, d, N_EXPERTS), d ** -0.5),
        "exp_w_gate": nrm((DEPTH, N_EXPERTS, d, D_EXPERT), d ** -0.5),
        "exp_w_up": nrm((DEPTH, N_EXPERTS, d, D_EXPERT), d ** -0.5),
        "exp_w_down": nrm((DEPTH, N_EXPERTS, D_EXPERT, d), D_EXPERT ** -0.5),
        "final_norm": gain((d,)),
    }


def reference(x_prompt, x_sample, cache_k_a, cache_v_a, state_hgrn, state_gla, c, c_ctx,
              ada_w, ada_b, norm_mix, norm_ffn, ab_w_in, ab_lambda, ab_subln, ab_lb_logits,
              ab_onorm, ab_w_out, c_w_in, c_gate_w2, c_gate_b, c_onorm, c_w_out,
              router_w, exp_w_gate, exp_w_up, exp_w_down, final_norm):
    lb_table = jnp.cumsum(jax.nn.softmax(ab_lb_logits.astype(F32), axis=0), axis=0)
    rope = axial_rope_tables(x_sample.shape[1])
    y_p, y_s = x_prompt, x_sample
    new_k, new_v, new_hgrn, new_gla = [], [], [], []
    for li in range(DEPTH):
        sh1_p, sc1_p, g1_p, sh2_p, sc2_p, g2_p = adaln(c_ctx[None, :], ada_w[li], ada_b[li])
        sh1_s, sc1_s, g1_s, sh2_s, sc2_s, g2_s = adaln(c, ada_w[li], ada_b[li])
        h_p = rmsnorm(y_p, norm_mix[li]) * (1 + sc1_p) + sh1_p
        h_s = rmsnorm(y_s, norm_mix[li]) * (1 + sc1_s) + sh1_s
        j = li // 2
        if li % 2 == 0:
            lam_init = 0.8 - 0.6 * math.exp(-0.3 * li)
            args = (ab_w_in[j], ab_lambda[j], ab_subln[j], lb_table[li], ab_onorm[j], ab_w_out[j], lam_init)
            m_p, (k_c, v_c, s_c) = ab_mixer(h_p, *args)
            m_s = ab_mixer(h_s, *args, rope=rope,
                           ctx=(cache_k_a[:, j], cache_v_a[:, j], state_hgrn[:, j]))
            new_k.append(k_c)
            new_v.append(v_c)
            new_hgrn.append(s_c)
        else:
            args = (c_w_in[j], c_gate_w2[j], c_gate_b[j], c_onorm[j], c_w_out[j])
            m_p, s_c = gla_mixer(h_p, *args)
            m_s = gla_mixer(h_s, *args, ctx_state=state_gla[:, j])
            new_gla.append(s_c)
        y_p = y_p + g1_p * m_p
        y_s = y_s + g1_s * m_s
        moe_args = (router_w[li], exp_w_gate[li], exp_w_up[li], exp_w_down[li])
        y_p = y_p + g2_p * ec_moe(rmsnorm(y_p, norm_ffn[li]) * (1 + sc2_p) + sh2_p, *moe_args)
        y_s = y_s + g2_s * ec_moe(rmsnorm(y_s, norm_ffn[li]) * (1 + sc2_s) + sh2_s, *moe_args)
    y_prompt = rmsnorm(y_p, final_norm)
    y_sample = rmsnorm(y_s, final_norm)
    return (y_prompt, y_sample, jnp.stack(new_k, axis=1), jnp.stack(new_v, axis=1),
            jnp.stack(new_hgrn, axis=1), jnp.stack(new_gla, axis=1))
```

```python
import functools
import math

import numpy as np
import jax
import jax.numpy as jnp
from jax import lax
from jax.experimental import pallas as pl
from jax.experimental.pallas import tpu as pltpu

D_MODEL = 4096
DEPTH = 2
GRID_W = 64
A_HEADS = 8
A_V = 256
A_QK = 128
A_WIDTH = A_HEADS * A_V
B_KEY = 128
B_VAL = 128
B_WIDTH = D_MODEL // 2
B_HEADS = B_WIDTH // B_KEY
C_HEADS = 8
C_KEY = D_MODEL // 2 // C_HEADS
C_VAL = D_MODEL // C_HEADS
C_RANK = 16
C_GATE_NORM = 16.0
N_EXPERTS = 16
EC_FACTOR = 2
D_EXPERT = D_MODEL // 2
CHUNK = 64
ROPE_BASE = 10000.0
EPS = 1e-6
F32 = jnp.float32
BF16 = jnp.bfloat16

LANES = 128
VMEM_LIMIT = 56 * 1024 * 1024
ROUTER_PAD = LANES


def _params(sem, vmem=VMEM_LIMIT):
    return pltpu.CompilerParams(dimension_semantics=sem, vmem_limit_bytes=vmem)


def _adaln_kernel(c_ref, w_ref, b_ref, o_ref):
    c = c_ref[...]
    a = (c * jax.nn.sigmoid(c)).astype(BF16)
    o_ref[...] = jnp.dot(a, w_ref[...].astype(BF16), preferred_element_type=F32) + b_ref[...]


def _adaln(cond, w, b):
    rows, d = cond.shape
    n = w.shape[1]
    tn = 512
    return pl.pallas_call(
        _adaln_kernel,
        out_shape=jax.ShapeDtypeStruct((rows, n), F32),
        grid=(n // tn,),
        in_specs=[pl.BlockSpec((rows, d), lambda j: (0, 0)),
                  pl.BlockSpec((d, tn), lambda j: (0, j)),
                  pl.BlockSpec((1, tn), lambda j: (0, j))],
        out_specs=pl.BlockSpec((rows, tn), lambda j: (0, j)),
        compiler_params=_params(("arbitrary",)),
        name="adaln",
    )(cond, w, b.reshape(1, n))


def _norm_kernel(*refs, has_resid, has_mod, has_router, emit_y, n_experts):
    it = iter(refs)
    y_ref = next(it)
    if has_resid:
        m_ref, gate_ref = next(it), next(it)
    g_ref = next(it)
    if has_mod:
        sc_ref, sh_ref = next(it), next(it)
    if has_router:
        rw_ref = next(it)
    if emit_y:
        yo_ref = next(it)
    h_ref = next(it)
    if has_router:
        aff_ref = next(it)

    x = y_ref[0]
    if has_resid:
        x = x + gate_ref[0] * m_ref[0]
    if emit_y:
        yo_ref[0] = x
    ms = jnp.mean(x * x, axis=-1, keepdims=True)
    h = (x * lax.rsqrt(ms + EPS)) * g_ref[...]
    if has_mod:
        h = h * (1.0 + sc_ref[0]) + sh_ref[0]
    h_ref[0] = h.astype(h_ref.dtype)
    if has_router:
        logits = jnp.dot(h, rw_ref[...], precision=lax.Precision.HIGHEST, preferred_element_type=F32)
        col = lax.broadcasted_iota(jnp.int32, logits.shape, 1)
        logits = jnp.where(col < n_experts, logits, -jnp.inf)
        e = jnp.exp(logits - jnp.max(logits, axis=-1, keepdims=True))
        aff_ref[0] = e / jnp.sum(e, axis=-1, keepdims=True)


def _norm(y, gnorm, *, resid=None, mod=None, router_w=None, emit_y=False, out_dtype=F32, tt=256):
    bsz, t, d = y.shape
    tt = min(tt, t)
    tok = pl.BlockSpec((1, tt, d), lambda b, i: (b, i, 0))

    def cond_spec(a):
        if a.shape[0] == 1:
            return pl.BlockSpec((1, 1, d), lambda b, i: (0, 0, 0))
        return pl.BlockSpec((1, 1, d), lambda b, i: (b, 0, 0))

    args, specs = [y], [tok]
    if resid is not None:
        args += [resid[0], resid[1]]
        specs += [tok, cond_spec(resid[1])]
    args.append(gnorm.reshape(1, d))
    specs.append(pl.BlockSpec((1, d), lambda b, i: (0, 0)))
    if mod is not None:
        args += [mod[0], mod[1]]
        specs += [cond_spec(mod[0]), cond_spec(mod[1])]
    if router_w is not None:
        rw = jnp.pad(router_w, ((0, 0), (0, ROUTER_PAD - router_w.shape[1])))
        args.append(rw)
        specs.append(pl.BlockSpec((d, ROUTER_PAD), lambda b, i: (0, 0)))
    out_shape, out_specs = [], []
    if emit_y:
        out_shape.append(jax.ShapeDtypeStruct((bsz, t, d), F32))
        out_specs.append(tok)
    out_shape.append(jax.ShapeDtypeStruct((bsz, t, d), out_dtype))
    out_specs.append(tok)
    if router_w is not None:
        out_shape.append(jax.ShapeDtypeStruct((bsz, t, ROUTER_PAD), F32))
        out_specs.append(pl.BlockSpec((1, tt, ROUTER_PAD), lambda b, i: (b, i, 0)))
    kern = functools.partial(_norm_kernel, has_resid=resid is not None, has_mod=mod is not None,
                             has_router=router_w is not None, emit_y=emit_y,
                             n_experts=0 if router_w is None else router_w.shape[1])
    return pl.pallas_call(
        kern, out_shape=out_shape, grid=(bsz, t // tt), in_specs=specs, out_specs=out_specs,
        compiler_params=_params(("arbitrary", "arbitrary")), name="norm",
    )(*args)


def _mm_kernel(*refs, n_in, has_resid):
    xs = refs[:n_in]
    ws = refs[n_in:2 * n_in]
    rest = refs[2 * n_in:]
    acc = jnp.dot(xs[0][...], ws[0][...], preferred_element_type=F32)
    for x_ref, w_ref in zip(xs[1:], ws[1:]):
        acc = acc + jnp.dot(x_ref[...], w_ref[...], preferred_element_type=F32)
    if has_resid:
        y_ref, gate_ref, o_ref = rest
        o_ref[...] = y_ref[...] + gate_ref[0] * acc
    else:
        (o_ref,) = rest
        o_ref[...] = acc.astype(o_ref.dtype)


def _mm(xs, w, *, n_cols=None, resid=None, rows_per_cond=None, out_dtype=F32, tm=1024, tn=1024):
    m = xs[0].shape[0]
    n = w.shape[1] if n_cols is None else n_cols
    tm, tn = min(tm, m), min(tn, n)
    if resid is not None and resid[1].shape[0] > 1:
        tm = min(tm, rows_per_cond)
    specs, args = [], []
    for x in xs:
        args.append(x)
        specs.append(pl.BlockSpec((tm, x.shape[1]), lambda i, j: (i, 0)))
    kb = 0
    for x in xs:
        k = x.shape[1]
        args.append(w)
        specs.append(pl.BlockSpec((k, tn), functools.partial(lambda i, j, r: (r, j), r=kb // k)))
        kb += k
    if resid is not None:
        y, gate = resid
        args += [y, gate]
        specs.append(pl.BlockSpec((tm, tn), lambda i, j: (i, j)))
        if gate.shape[0] == 1:
            specs.append(pl.BlockSpec((1, 1, tn), lambda i, j: (0, 0, j)))
        else:
            per = rows_per_cond // tm
            specs.append(pl.BlockSpec((1, 1, tn), lambda i, j: (i // per, 0, j)))
    kern = functools.partial(_mm_kernel, n_in=len(xs), has_resid=resid is not None)
    return pl.pallas_call(
        kern, out_shape=jax.ShapeDtypeStruct((m, n), out_dtype), grid=(m // tm, n // tn),
        in_specs=specs, out_specs=pl.BlockSpec((tm, tn), lambda i, j: (i, j)),
        compiler_params=_params(("arbitrary", "arbitrary")), name="matmul",
    )(*args)


def _rope_tables(n_tokens):
    rows = n_tokens // GRID_W
    row = jnp.repeat(jnp.arange(rows, dtype=F32), GRID_W)
    col = jnp.tile(jnp.arange(GRID_W, dtype=F32), rows)
    axis_dim = A_QK // 2
    inv = ROPE_BASE ** (-jnp.arange(0, axis_dim, 2, dtype=F32) / axis_dim)
    ar, ac = row[:, None] * inv, col[:, None] * inv
    cos = jnp.concatenate([jnp.cos(ar), jnp.cos(ar), jnp.cos(ac), jnp.cos(ac)], axis=-1)
    sin = jnp.concatenate([-jnp.sin(ar), jnp.sin(ar), -jnp.sin(ac), jnp.sin(ac)], axis=-1)
    return cos, sin


def _rope(x, cos, sin):
    q = A_QK // 4
    lane = lax.broadcasted_iota(jnp.int32, x.shape, 1)
    first = (lane % (2 * q)) < q
    partner = jnp.where(first, pltpu.roll(x, A_QK - q, axis=1), pltpu.roll(x, q, axis=1))
    return x * cos + partner * sin


def _attn_kernel(*refs, use_ctx, emit_kv, lam_init, t_own):
    it = iter(refs)
    q_ref, k_ref, v_ref = next(it), next(it), next(it)
    if use_ctx:
        ck_ref, cv_ref, cq_ref, sq_ref, ckk_ref, skk_ref = (next(it) for _ in range(6))
    lam_ref, sub_ref = next(it), next(it)
    o_ref = next(it)
    if emit_kv:
        nk_ref, nv_ref = next(it), next(it)
    kb_ref, vb_ref = next(it), next(it)

    @pl.when(pl.program_id(2) == 0)
    def _():
        k = k_ref[0]
        v = v_ref[0]
        if emit_kv:
            nk_ref[0] = k
            nv_ref[0] = v
        for mp in range(2):
            km = k[:, mp * A_QK:(mp + 1) * A_QK]
            if use_ctx:
                km = _rope(km, ckk_ref[...], skk_ref[...])
            kb_ref[0:t_own, mp * A_QK:(mp + 1) * A_QK] = km.astype(BF16)
        vb_ref[0:t_own, :] = v.astype(BF16)
        if use_ctx:
            kb_ref[t_own:, :] = ck_ref[0].astype(BF16)
            vb_ref[t_own:, :] = cv_ref[0].astype(BF16)

    lp = lam_ref[...]
    lam = (jnp.exp(jnp.sum(lp[0:1] * lp[1:2], axis=-1, keepdims=True))
           - jnp.exp(jnp.sum(lp[2:3] * lp[3:4], axis=-1, keepdims=True)) + lam_init)
    q = q_ref[0]
    scale = A_QK ** -0.5
    es, ls = [], []
    for mp in range(2):
        qm = q[:, mp * A_QK:(mp + 1) * A_QK]
        if use_ctx:
            qm = _rope(qm, cq_ref[...], sq_ref[...])
        qm = (qm * scale).astype(BF16)
        s = lax.dot_general(qm, kb_ref[:, mp * A_QK:(mp + 1) * A_QK], (((1,), (1,)), ((), ())),
                            preferred_element_type=F32)
        e = jnp.exp(s - jnp.max(s, axis=-1, keepdims=True))
        es.append(e)
        ls.append(jnp.sum(e, axis=-1, keepdims=True))
    w = es[0] * (1.0 / ls[0]) - es[1] * (lam / ls[1])
    o = jnp.dot(w.astype(BF16), vb_ref[...], preferred_element_type=F32)
    o = o * lax.rsqrt(jnp.mean(o * o, axis=-1, keepdims=True) + EPS) * sub_ref[...]
    o_ref[0] = (o * (1.0 - lam_init)).astype(o_ref.dtype)


def _attention(proj, lam_p, subln, lam_init, ctx=None, rope=None, tq=256):
    bsz, t, _ = proj.shape
    tq = min(tq, t)
    hw = 2 * A_QK
    use_ctx = ctx is not None
    args = [proj, proj, proj]
    specs = [pl.BlockSpec((1, tq, hw), lambda b, h, i: (b, i, h)),
             pl.BlockSpec((1, t, hw), lambda b, h, i: (b, 0, A_HEADS + h)),
             pl.BlockSpec((1, t, A_V), lambda b, h, i: (b, 0, 2 * A_HEADS + h))]
    t_all = t
    if use_ctx:
        ck, cv = ctx
        p = ck.shape[1]
        t_all = t + p
        cos, sin = rope
        args += [ck.reshape(bsz, p, A_HEADS * hw), cv.reshape(bsz, p, A_WIDTH), cos, sin, cos, sin]
        specs += [pl.BlockSpec((1, p, hw), lambda b, h, i: (b, 0, h)),
                  pl.BlockSpec((1, p, A_V), lambda b, h, i: (b, 0, h)),
                  pl.BlockSpec((tq, A_QK), lambda b, h, i: (i, 0)),
                  pl.BlockSpec((tq, A_QK), lambda b, h, i: (i, 0)),
                  pl.BlockSpec((t, A_QK), lambda b, h, i: (0, 0)),
                  pl.BlockSpec((t, A_QK), lambda b, h, i: (0, 0))]
    args += [lam_p, subln.reshape(1, A_V)]
    specs += [pl.BlockSpec((4, A_QK), lambda b, h, i: (0, 0)),
              pl.BlockSpec((1, A_V), lambda b, h, i: (0, 0))]
    out_shape = [jax.ShapeDtypeStruct((bsz, t, A_WIDTH), BF16)]
    out_specs = [pl.BlockSpec((1, tq, A_V), lambda b, h, i: (b, i, h))]
    if not use_ctx:
        out_shape += [jax.ShapeDtypeStruct((bsz, t, A_WIDTH), F32)] * 2
        out_specs += [pl.BlockSpec((1, t, hw), lambda b, h, i: (b, 0, h)),
                      pl.BlockSpec((1, t, A_V), lambda b, h, i: (b, 0, h))]
    kern = functools.partial(_attn_kernel, use_ctx=use_ctx, emit_kv=not use_ctx, lam_init=lam_init, t_own=t)
    return pl.pallas_call(
        kern, out_shape=out_shape, grid=(bsz, A_HEADS, t // tq), in_specs=specs, out_specs=out_specs,
        scratch_shapes=[pltpu.VMEM((t_all, hw), BF16), pltpu.VMEM((t_all, A_V), BF16)],
        compiler_params=_params(("arbitrary", "arbitrary", "arbitrary")), name="diff_attention",
    )(*args)


def _level_table(rev):
    i = np.arange(CHUNK)[:, None]
    j = np.arange(CHUNK)[None, :]
    x = i ^ j
    lvl = np.where(x == 0, 0, 2 ** np.floor(np.log2(np.maximum(x, 1))).astype(np.int64))
    keep = (i <= j) if rev else (i >= j)
    return jnp.asarray(np.where(keep, lvl, -1), dtype=jnp.int32)


def _scan_chunk(q, k, v, g, st, lvl, rev):
    c, dk = g.shape
    row = lax.broadcasted_iota(jnp.int32, (c, dk), 0)
    pos = (c - 1 - row) if rev else row

    def prev(x, d):
        return pltpu.roll(x, (c - d) if rev else d, axis=0)

    def nxt(x, d):
        return pltpu.roll(x, d if rev else (c - d), axis=0)

    b = g
    d = 1
    while d < c:
        b = b + jnp.where(pos >= d, prev(b, d), 0.0)
        d *= 2
    u1 = nxt(g, 1)
    u2 = u1 + nxt(g, 2)
    u3 = u2 + nxt(g, 3)
    l1 = g + prev(g, 1)
    l2 = l1 + prev(g, 2)
    l3 = l2 + prev(g, 3)
    p8 = pos % 8
    args = {
        1: jnp.where(pos % 2 == 1, g, 0.0),
        2: jnp.where(pos % 4 == 0, u1, jnp.where(pos % 4 == 1, 0.0, jnp.where(pos % 4 == 2, g, l1))),
        4: jnp.where(p8 == 0, u3, jnp.where(p8 == 1, u2, jnp.where(p8 == 2, u1, jnp.where(
            p8 == 3, 0.0, jnp.where(p8 == 4, g, jnp.where(p8 == 5, l1, jnp.where(p8 == 6, l2, l3))))))),
    }
    s = 8
    while s < c:
        parts = []
        for blk in range(c // (2 * s)):
            m = blk * 2 * s + (s if rev else s - 1)
            parts.append(jnp.broadcast_to(b[m:m + 1, :], (2 * s, dk)))
        bref = jnp.concatenate(parts, axis=0) if len(parts) > 1 else parts[0]
        args[s] = jnp.where(pos % (2 * s) >= s, b - bref, bref - b)
        s *= 2

    nt = (((1,), (1,)), ((), ()))
    a = jnp.where(lvl == 0, lax.dot_general(q.astype(BF16), k.astype(BF16), nt, preferred_element_type=F32), 0.0)
    for s, arg in args.items():
        e = jnp.exp(jnp.minimum(arg, 0.0))
        a_s = lax.dot_general((q * e).astype(BF16), (k * e).astype(BF16), nt, preferred_element_type=F32)
        a = jnp.where(lvl == s, a_s, a)

    last = 0 if rev else c - 1
    b_last = b[last:last + 1, :]
    o = lax.dot_general((q * jnp.exp(b)).astype(BF16), st.astype(BF16), nt, preferred_element_type=F32)
    o = o + jnp.dot(a.astype(BF16), v.astype(BF16), preferred_element_type=F32)
    k_dec = (k * jnp.exp(b_last - b)).astype(BF16)
    st_new = jnp.exp(b_last) * st + lax.dot_general(v.astype(BF16), k_dec, (((0,), (0,)), ((), ())),
                                                    preferred_element_type=F32)
    return o, st_new


def _scan_kernel(*refs, mode, has_s0, emit_state, t, dk, dv):
    it = iter(refs)
    q_ref = next(it)
    if mode == "hgrn":
        zf_ref, zb_ref, v_ref, og_ref, lbf_ref, lbb_ref = (next(it) for _ in range(6))
    else:
        k_ref, v_ref, og_ref, lr_ref, w2f_ref, w2b_ref, gbf_ref, gbb_ref = (next(it) for _ in range(8))
    if has_s0:
        s0f_ref, s0b_ref = next(it), next(it)
    onorm_ref, lvf_ref, lvb_ref = next(it), next(it), next(it)
    o_ref = next(it)
    if emit_state:
        sf_ref, sb_ref = next(it), next(it)
    oacc_ref, st_ref = next(it), next(it)
    n = t // CHUNK

    def load(ci, rev):
        r0 = pl.multiple_of(ci * CHUNK, CHUNK)
        rows = pl.ds(r0, CHUNK)
        q = q_ref[0, rows, :]
        v = v_ref[0, rows, :]
        if mode == "hgrn":
            z = (zb_ref if rev else zf_ref)[0, rows, :]
            lb = (lbb_ref if rev else lbf_ref)[...]
            f = lb + (1.0 - lb) * jax.nn.sigmoid(z)
            k = 1.0 - f
            g = jnp.log(f)
        else:
            k = k_ref[0, rows, :]
            q = q * (dk ** -0.5)
            x = jnp.dot(lr_ref[0, rows, :].astype(BF16), (w2b_ref if rev else w2f_ref)[...],
                        preferred_element_type=F32) + (gbb_ref if rev else gbf_ref)[...]
            g = jax.nn.log_sigmoid(x) / C_GATE_NORM
        return rows, q, k, v, g

    for rev in (False, True):
        if has_s0:
            st_ref[...] = (s0b_ref if rev else s0f_ref)[0, 0, 0].T
        else:
            st_ref[...] = jnp.zeros_like(st_ref)
        lvl = (lvb_ref if rev else lvf_ref)[...]

        def body(step, carry, rev=rev, lvl=lvl):
            ci = (n - 1 - step) if rev else step
            rows, q, k, v, g = load(ci, rev)
            o, st_new = _scan_chunk(q, k, v, g, st_ref[...], lvl, rev)
            st_ref[...] = st_new
            if not rev:
                oacc_ref[rows, :] = o
            else:
                o = o + oacc_ref[rows, :]
                o = o * lax.rsqrt(jnp.mean(o * o, axis=-1, keepdims=True) + EPS) * onorm_ref[...]
                og = og_ref[0, rows, :]
                o_ref[0, rows, :] = (o * (og * jax.nn.sigmoid(og))).astype(o_ref.dtype)
            return carry

        lax.fori_loop(0, n, body, 0)
        if emit_state:
            (sb_ref if rev else sf_ref)[0, 0] = st_ref[...].T


def _scan(mode, proj, col0, onorm, *, heads, dk, dv, extra, s0=None, emit_state=False):
    bsz, t, _ = proj.shape

    def col(name, width):
        base = col0[name] // width
        return pl.BlockSpec((1, t, width), lambda b, h: (b, 0, base + h))

    args, specs = [proj], [col("q", dk)]
    if mode == "hgrn":
        lbf, lbb = extra
        args += [proj, proj, proj, proj, lbf, lbb]
        specs += [col("zf", dk), col("zb", dk), col("v", dv), col("og", dv),
                  pl.BlockSpec((1, dk), lambda b, h: (0, h)), pl.BlockSpec((1, dk), lambda b, h: (0, h))]
    else:
        lr, w2f, w2b, gbf, gbb = extra
        args += [proj, proj, proj, lr, w2f, w2b, gbf, gbb]
        specs += [col("k", dk), col("v", dv), col("og", dv),
                  pl.BlockSpec((1, t, LANES), lambda b, h: (b, 0, 0)),
                  pl.BlockSpec((LANES, dk), lambda b, h: (0, h)), pl.BlockSpec((LANES, dk), lambda b, h: (0, h)),
                  pl.BlockSpec((1, dk), lambda b, h: (0, h)), pl.BlockSpec((1, dk), lambda b, h: (0, h))]
    st_block = (1, 1, 1, dk, dv)
    if s0 is not None:
        args += [s0, s0]
        specs += [pl.BlockSpec(st_block, lambda b, h: (b, 0, h, 0, 0)),
                  pl.BlockSpec(st_block, lambda b, h: (b, 1, h, 0, 0))]
    args += [onorm.reshape(1, dv), _level_table(False), _level_table(True)]
    specs += [pl.BlockSpec((1, dv), lambda b, h: (0, 0)),
              pl.BlockSpec((CHUNK, CHUNK), lambda b, h: (0, 0)),
              pl.BlockSpec((CHUNK, CHUNK), lambda b, h: (0, 0))]
    out_shape = [jax.ShapeDtypeStruct((bsz, t, heads * dv), BF16)]
    out_specs = [pl.BlockSpec((1, t, dv), lambda b, h: (b, 0, h))]
    if emit_state:
        out_shape += [jax.ShapeDtypeStruct((bsz, heads, dk, dv), F32)] * 2
        out_specs += [pl.BlockSpec((1, 1, dk, dv), lambda b, h: (b, h, 0, 0))] * 2
    kern = functools.partial(_scan_kernel, mode=mode, has_s0=s0 is not None, emit_state=emit_state,
                             t=t, dk=dk, dv=dv)
    return pl.pallas_call(
        kern, out_shape=out_shape, grid=(bsz, heads), in_specs=specs, out_specs=out_specs,
        scratch_shapes=[pltpu.VMEM((t, dv), F32), pltpu.VMEM((dv, dk), F32)],
        compiler_params=_params(("arbitrary", "arbitrary")), name="gated_scan_" + mode,
    )(*args)


def _moe_kernel(idx_ref, h_hbm, gate_ref, wg_ref, wu_ref, wd_ref, zero_hbm, out_hbm,
                row_ref, xb_ref, acc_ref, sem_x, sem_o, *, tm, cap):
    del zero_hbm
    e, m, f = pl.program_id(0), pl.program_id(1), pl.program_id(2)
    base = e * cap + m * tm

    def x_copy(r):
        tok = idx_ref[base + r]
        return pltpu.make_async_copy(h_hbm.at[pl.ds(tok, 1)], row_ref.at[pl.ds(r, 1)], sem_x)

    def o_gather(r):
        tok = idx_ref[base + r]
        return pltpu.make_async_copy(out_hbm.at[pl.ds(tok, 1)], row_ref.at[pl.ds(r, 1)], sem_o)

    def o_scatter(r):
        tok = idx_ref[base + r]
        return pltpu.make_async_copy(row_ref.at[pl.ds(r, 1)], out_hbm.at[pl.ds(tok, 1)], sem_o)

    def for_rows(fn):
        def body(r, c):
            fn(r)
            return c
        lax.fori_loop(0, tm, body, 0)

    @pl.when(f == 0)
    def _():
        for_rows(lambda r: x_copy(r).start())
        for_rows(lambda r: x_copy(r).wait())
        xb_ref[...] = row_ref[...].astype(BF16)
        for_rows(lambda r: o_gather(r).start())

    x = xb_ref[...]
    g = jnp.dot(x, wg_ref[0], preferred_element_type=F32)
    u = jnp.dot(x, wu_ref[0], preferred_element_type=F32)
    hid = (g * jax.nn.sigmoid(g) * u).astype(BF16)
    part = jnp.dot(hid, wd_ref[0], preferred_element_type=F32)

    @pl.when(f == 0)
    def _():
        acc_ref[...] = part

    @pl.when(f > 0)
    def _():
        acc_ref[...] += part

    @pl.when(f == pl.num_programs(2) - 1)
    def _():
        for_rows(lambda r: o_gather(r).wait())
        row_ref[...] += acc_ref[...] * gate_ref[0]
        for_rows(lambda r: o_scatter(r).start())
        for_rows(lambda r: o_scatter(r).wait())


def _moe(h, idx, gate, wg, wu, wd, tm=512, tf=256):
    n, d = h.shape
    n_e, cap = idx.shape
    dexp = wg.shape[2]
    tm, tf = min(tm, cap), min(tf, dexp)
    grid_spec = pltpu.PrefetchScalarGridSpec(
        num_scalar_prefetch=1,
        grid=(n_e, cap // tm, dexp // tf),
        in_specs=[pl.BlockSpec(memory_space=pl.ANY),
                  pl.BlockSpec((1, tm, 1), lambda e, m, f, idx: (e, m, 0)),
                  pl.BlockSpec((1, d, tf), lambda e, m, f, idx: (e, 0, f)),
                  pl.BlockSpec((1, d, tf), lambda e, m, f, idx: (e, 0, f)),
                  pl.BlockSpec((1, tf, d), lambda e, m, f, idx: (e, f, 0)),
                  pl.BlockSpec(memory_space=pl.ANY)],
        out_specs=pl.BlockSpec(memory_space=pl.ANY),
        scratch_shapes=[pltpu.VMEM((tm, d), F32), pltpu.VMEM((tm, d), BF16), pltpu.VMEM((tm, d), F32),
                        pltpu.SemaphoreType.DMA, pltpu.SemaphoreType.DMA],
    )
    return pl.pallas_call(
        functools.partial(_moe_kernel, tm=tm, cap=cap),
        out_shape=jax.ShapeDtypeStruct((n, d), F32), grid_spec=grid_spec,
        input_output_aliases={6: 0},
        compiler_params=_params(("arbitrary", "arbitrary", "arbitrary")), name="moe_ffn",
    )(idx.reshape(-1), h, gate.reshape(n_e, cap, 1), wg, wu, wd, jnp.zeros((n, d), F32))


def _mods(ada, lo, hi):
    d = ada.shape[1] // 6
    return [ada[lo:hi, k * d:(k + 1) * d].reshape(hi - lo, 1, d) for k in range(6)]


def _ab_mixer(h, w_in, lam_p, subln, lb, onorm, lam_init, rope=None, ctx=None):
    bsz, t, d = h.shape
    proj = _mm([h.reshape(bsz * t, d)], w_in).reshape(bsz, t, -1)
    bw = B_HEADS * B_KEY
    col0 = {"q": 3 * A_WIDTH, "zf": 3 * A_WIDTH + bw, "zb": 3 * A_WIDTH + 2 * bw,
            "v": 3 * A_WIDTH + 3 * bw, "og": 3 * A_WIDTH + 4 * bw}
    extra = (lb[0:1], lb[1:2])
    if ctx is None:
        o_a, k_c, v_c = _attention(proj, lam_p, subln, lam_init)
        o_b, s_f, s_b = _scan("hgrn", proj, col0, onorm, heads=B_HEADS, dk=B_KEY, dv=B_VAL, extra=extra,
                              emit_state=True)
        return (o_a, o_b), (k_c, v_c, jnp.stack([s_f, s_b], axis=1))
    ck, cv, cs = ctx
    (o_a,) = _attention(proj, lam_p, subln, lam_init, ctx=(ck, cv), rope=rope)
    (o_b,) = _scan("hgrn", proj, col0, onorm, heads=B_HEADS, dk=B_KEY, dv=B_VAL, extra=extra, s0=cs)
    return (o_a, o_b), None


def _gla_mixer(h, w_main, w_lr, w2f, w2b, gbf, gbb, onorm, ctx_state=None):
    bsz, t, d = h.shape
    h2d = h.reshape(bsz * t, d)
    n_main = C_HEADS * (2 * C_KEY + 2 * C_VAL)
    proj = _mm([h2d], w_main, n_cols=n_main).reshape(bsz, t, n_main)
    lr = _mm([h2d], w_lr).reshape(bsz, t, LANES)
    col0 = {"q": 0, "k": C_HEADS * C_KEY, "v": 2 * C_HEADS * C_KEY, "og": 2 * C_HEADS * C_KEY + C_HEADS * C_VAL}
    extra = (lr, w2f, w2b, gbf, gbb)
    if ctx_state is None:
        o, s_f, s_b = _scan("gla", proj, col0, onorm, heads=C_HEADS, dk=C_KEY, dv=C_VAL, extra=extra,
                            emit_state=True)
        return (o,), jnp.stack([s_f, s_b], axis=1)
    (o,) = _scan("gla", proj, col0, onorm, heads=C_HEADS, dk=C_KEY, dv=C_VAL, extra=extra, s0=ctx_state)
    return (o,), None


def kernel(x_prompt, x_sample, cache_k_a, cache_v_a, state_hgrn, state_gla, c, c_ctx, ada_w, ada_b, norm_mix, norm_ffn, ab_w_in, ab_lambda, ab_subln, ab_lb_logits, ab_onorm, ab_w_out, c_w_in, c_gate_w2, c_gate_b, c_onorm, c_w_out, router_w, exp_w_gate, exp_w_up, exp_w_down, final_norm):
    d = x_prompt.shape[-1]
    n_s = c.shape[0]
    lb_table = jnp.cumsum(jax.nn.softmax(ab_lb_logits.astype(F32), axis=0), axis=0)
    rope = _rope_tables(x_sample.shape[1])
    cond_rows = 8 * ((1 + n_s + 7) // 8)
    cond = jnp.concatenate([c_ctx[None, :], c, jnp.zeros((cond_rows - 1 - n_s, d), F32)], axis=0)

    ys = [x_prompt, x_sample]
    pending = [None, None]
    new_k = new_v = new_hgrn = new_gla = None
    for li in range(DEPTH):
        ada = _adaln(cond, ada_w[li], ada_b[li])
        mods = [_mods(ada, 0, 1), _mods(ada, 1, 1 + n_s)]
        j = li // 2
        if li % 2 == 0:
            lam_init = 0.8 - 0.6 * math.exp(-0.3 * li)
            w_in = ab_w_in[j].astype(BF16)
            w_out = ab_w_out[j].astype(BF16)
        else:
            n_main = C_HEADS * (2 * C_KEY + 2 * C_VAL)
            w_in = c_w_in[j].astype(BF16)
            w_lr = jnp.pad(c_w_in[j][:, n_main:], ((0, 0), (0, LANES - 2 * C_RANK))).astype(BF16)
            w2 = c_gate_w2[j]
            w2f = jnp.pad(w2[0], ((0, LANES - C_RANK), (0, 0))).astype(BF16)
            w2b = jnp.pad(w2[1], ((C_RANK, LANES - 2 * C_RANK), (0, 0))).astype(BF16)
            gbf, gbb = c_gate_b[j][0:1], c_gate_b[j][1:2]
            w_out = c_w_out[j].astype(BF16)
        wg, wu, wd = exp_w_gate[li].astype(BF16), exp_w_up[li].astype(BF16), exp_w_down[li].astype(BF16)
        for gi in range(2):
            sh1, sc1, g1, sh2, sc2, g2 = mods[gi]
            y = ys[gi]
            bsz, t, _ = y.shape
            if pending[gi] is None:
                (h,) = _norm(y, norm_mix[li], mod=(sc1, sh1), out_dtype=BF16)
            else:
                y, h = _norm(y, norm_mix[li], resid=pending[gi], mod=(sc1, sh1), emit_y=True, out_dtype=BF16)
            is_ctx = gi == 0
            if li % 2 == 0:
                ctx = None if is_ctx else (cache_k_a[:, j], cache_v_a[:, j], state_hgrn[:, j])
                outs, cache = _ab_mixer(h, w_in, ab_lambda[j], ab_subln[j], lb_table[li], ab_onorm[j], lam_init,
                                        rope=rope, ctx=ctx)
                if is_ctx:
                    new_k, new_v, new_hgrn = cache
            else:
                outs, cache = _gla_mixer(h, w_in, w_lr, w2f, w2b, gbf, gbb, c_onorm[j],
                                         ctx_state=None if is_ctx else state_gla[:, j])
                if is_ctx:
                    new_gla = cache
            xs = [o.reshape(bsz * t, -1) for o in outs]
            y1 = _mm(xs, w_out, resid=(y.reshape(bsz * t, d), g1), rows_per_cond=t).reshape(bsz, t, d)
            h2, aff = _norm(y1, norm_ffn[li], mod=(sc2, sh2), router_w=router_w[li], out_dtype=F32)
            n_tok = bsz * t
            n_e = router_w.shape[-1]
            cap = EC_FACTOR * n_tok // n_e
            gate, idx = lax.top_k(aff.reshape(n_tok, ROUTER_PAD)[:, :n_e].T, cap)
            moe = _moe(h2.reshape(n_tok, d), idx, gate, wg, wu, wd).reshape(bsz, t, d)
            ys[gi] = y1
            pending[gi] = (moe, g2)
    outs = []
    for gi in range(2):
        (yo,) = _norm(ys[gi], final_norm, resid=pending[gi], out_dtype=F32)
        outs.append(yo)
    bp, tp = x_prompt.shape[:2]
    return (outs[0], outs[1],
            new_k.reshape(bp, tp, A_HEADS, 2 * A_QK)[:, None],
            new_v.reshape(bp, tp, A_HEADS, A_V)[:, None],
            new_hgrn[:, None], new_gla[:, None])
```

```python
import functools
import math

import numpy as np
import jax
import jax.numpy as jnp
from jax import lax
from jax.experimental import pallas as pl
from jax.experimental.pallas import tpu as pltpu

D_MODEL = 4096
DEPTH = 2
GRID_W = 64
A_HEADS = 8
A_V = 256
A_QK = 128
A_WIDTH = A_HEADS * A_V
B_KEY = 128
B_VAL = 128
B_WIDTH = D_MODEL // 2
B_HEADS = B_WIDTH // B_KEY
C_HEADS = 8
C_KEY = D_MODEL // 2 // C_HEADS
C_VAL = D_MODEL // C_HEADS
C_RANK = 16
C_GATE_NORM = 16.0
N_EXPERTS = 16
EC_FACTOR = 2
D_EXPERT = D_MODEL // 2
CHUNK = 64
ROPE_BASE = 10000.0
EPS = 1e-6
LOG2E = math.log2(math.e)
F32 = jnp.float32
BF16 = jnp.bfloat16

LANES = 128
VMEM_LIMIT = 56 * 1024 * 1024
ROUTER_PAD = LANES


def _params(sem, vmem=VMEM_LIMIT):
    return pltpu.CompilerParams(dimension_semantics=sem, vmem_limit_bytes=vmem)


def _adaln_kernel(c_ref, w_ref, b_ref, o_ref):
    c = c_ref[...]
    a = (c * jax.nn.sigmoid(c)).astype(BF16)
    o_ref[...] = jnp.dot(a, w_ref[...].astype(BF16), preferred_element_type=F32) + b_ref[...]


def _adaln(cond, w, b, li):
    rows, d = cond.shape
    n_layers, _, n = w.shape
    tn = 512
    return pl.pallas_call(
        _adaln_kernel,
        out_shape=jax.ShapeDtypeStruct((rows, n), F32),
        grid=(n // tn,),
        in_specs=[pl.BlockSpec((rows, d), lambda j: (0, 0)),
                  pl.BlockSpec((None, d, tn), lambda j: (li, 0, j)),
                  pl.BlockSpec((None, 1, tn), lambda j: (li, 0, j))],
        out_specs=pl.BlockSpec((rows, tn), lambda j: (0, j)),
        compiler_params=_params(("arbitrary",)),
        name="adaln",
    )(cond, w, b.reshape(n_layers, 1, n))


def _norm_kernel(*refs, has_resid, has_mod, has_router, emit_y, n_experts):
    it = iter(refs)
    y_ref = next(it)
    if has_resid:
        m_ref, gate_ref = next(it), next(it)
    g_ref = next(it)
    if has_mod:
        sc_ref, sh_ref = next(it), next(it)
    if has_router:
        rw_ref = next(it)
    if emit_y:
        yo_ref = next(it)
    h_ref = next(it)
    if has_router:
        aff_ref = next(it)

    x = y_ref[0]
    if has_resid:
        x = x + gate_ref[0] * m_ref[0]
    if emit_y:
        yo_ref[0] = x
    ms = jnp.mean(x * x, axis=-1, keepdims=True)
    h = (x * lax.rsqrt(ms + EPS)) * g_ref[...]
    if has_mod:
        h = h * (1.0 + sc_ref[0]) + sh_ref[0]
    h_ref[0] = h.astype(h_ref.dtype)
    if has_router:
        logits = jnp.dot(h, rw_ref[...], precision=lax.Precision.HIGHEST, preferred_element_type=F32)
        col = lax.broadcasted_iota(jnp.int32, logits.shape, 1)
        logits = jnp.where(col < n_experts, logits, -jnp.inf)
        e = jnp.exp(logits - jnp.max(logits, axis=-1, keepdims=True))
        aff_ref[0] = e / jnp.sum(e, axis=-1, keepdims=True)


def _norm(y, gnorm, *, resid=None, mod=None, router_w=None, emit_y=False, out_dtype=F32, tt=256):
    bsz, t, d = y.shape
    tt = min(tt, t)
    tok = pl.BlockSpec((1, tt, d), lambda b, i: (b, i, 0))

    def cond_spec(a):
        if a.shape[0] == 1:
            return pl.BlockSpec((1, 1, d), lambda b, i: (0, 0, 0))
        return pl.BlockSpec((1, 1, d), lambda b, i: (b, 0, 0))

    args, specs = [y], [tok]
    if resid is not None:
        args += [resid[0], resid[1]]
        specs += [tok, cond_spec(resid[1])]
    args.append(gnorm.reshape(1, d))
    specs.append(pl.BlockSpec((1, d), lambda b, i: (0, 0)))
    if mod is not None:
        args += [mod[0], mod[1]]
        specs += [cond_spec(mod[0]), cond_spec(mod[1])]
    if router_w is not None:
        rw = jnp.pad(router_w, ((0, 0), (0, ROUTER_PAD - router_w.shape[1])))
        args.append(rw)
        specs.append(pl.BlockSpec((d, ROUTER_PAD), lambda b, i: (0, 0)))
    out_shape, out_specs = [], []
    if emit_y:
        out_shape.append(jax.ShapeDtypeStruct((bsz, t, d), F32))
        out_specs.append(tok)
    out_shape.append(jax.ShapeDtypeStruct((bsz, t, d), out_dtype))
    out_specs.append(tok)
    if router_w is not None:
        out_shape.append(jax.ShapeDtypeStruct((bsz, t, ROUTER_PAD), F32))
        out_specs.append(pl.BlockSpec((1, tt, ROUTER_PAD), lambda b, i: (b, i, 0)))
    kern = functools.partial(_norm_kernel, has_resid=resid is not None, has_mod=mod is not None,
                             has_router=router_w is not None, emit_y=emit_y,
                             n_experts=0 if router_w is None else router_w.shape[1])
    return pl.pallas_call(
        kern, out_shape=out_shape, grid=(bsz, t // tt), in_specs=specs, out_specs=out_specs,
        compiler_params=_params(("arbitrary", "arbitrary")), name="norm",
    )(*args)


def _mm_kernel(*refs, n_in, has_resid):
    xs = refs[:n_in]
    ws = refs[n_in:2 * n_in]
    rest = refs[2 * n_in:]
    acc = jnp.dot(xs[0][...], ws[0][...], preferred_element_type=F32)
    for x_ref, w_ref in zip(xs[1:], ws[1:]):
        acc = acc + jnp.dot(x_ref[...], w_ref[...], preferred_element_type=F32)
    if has_resid:
        y_ref, gate_ref, o_ref = rest
        o_ref[...] = y_ref[...] + gate_ref[0] * acc
    else:
        (o_ref,) = rest
        o_ref[...] = acc.astype(o_ref.dtype)


def _mm(xs, w, *, n_cols=None, resid=None, rows_per_cond=None, out_dtype=F32, tm=1024, tn=1024):
    m = xs[0].shape[0]
    n = w.shape[1] if n_cols is None else n_cols
    tm, tn = min(tm, m), min(tn, n)
    if resid is not None and resid[1].shape[0] > 1:
        tm = min(tm, rows_per_cond)
    specs, args = [], []
    for x in xs:
        args.append(x)
        specs.append(pl.BlockSpec((tm, x.shape[1]), lambda i, j: (i, 0)))
    kb = 0
    for x in xs:
        k = x.shape[1]
        args.append(w)
        specs.append(pl.BlockSpec((k, tn), functools.partial(lambda i, j, r: (r, j), r=kb // k)))
        kb += k
    if resid is not None:
        y, gate = resid
        args += [y, gate]
        specs.append(pl.BlockSpec((tm, tn), lambda i, j: (i, j)))
        if gate.shape[0] == 1:
            specs.append(pl.BlockSpec((1, 1, tn), lambda i, j: (0, 0, j)))
        else:
            per = rows_per_cond // tm
            specs.append(pl.BlockSpec((1, 1, tn), lambda i, j: (i // per, 0, j)))
    kern = functools.partial(_mm_kernel, n_in=len(xs), has_resid=resid is not None)
    return pl.pallas_call(
        kern, out_shape=jax.ShapeDtypeStruct((m, n), out_dtype), grid=(m // tm, n // tn),
        in_specs=specs, out_specs=pl.BlockSpec((tm, tn), lambda i, j: (i, j)),
        compiler_params=_params(("arbitrary", "arbitrary")), name="matmul",
    )(*args)


def _rope_tables(n_tokens):
    rows = n_tokens // GRID_W
    row = jnp.repeat(jnp.arange(rows, dtype=F32), GRID_W)
    col = jnp.tile(jnp.arange(GRID_W, dtype=F32), rows)
    axis_dim = A_QK // 2
    inv = ROPE_BASE ** (-jnp.arange(0, axis_dim, 2, dtype=F32) / axis_dim)
    ar, ac = row[:, None] * inv, col[:, None] * inv
    cos = jnp.concatenate([jnp.cos(ar), jnp.cos(ar), jnp.cos(ac), jnp.cos(ac)], axis=-1)
    sin = jnp.concatenate([-jnp.sin(ar), jnp.sin(ar), -jnp.sin(ac), jnp.sin(ac)], axis=-1)
    return cos, sin


def _rope(x, cos, sin):
    q = A_QK // 4
    lane = lax.broadcasted_iota(jnp.int32, x.shape, 1)
    first = (lane % (2 * q)) < q
    partner = jnp.where(first, pltpu.roll(x, A_QK - q, axis=1), pltpu.roll(x, q, axis=1))
    return x * cos + partner * sin


def _attn_kernel(*refs, use_ctx, emit_kv, lam_init, t_own):
    it = iter(refs)
    q_ref, k_ref, v_ref = next(it), next(it), next(it)
    if use_ctx:
        ck_ref, cv_ref, cq_ref, sq_ref, ckk_ref, skk_ref = (next(it) for _ in range(6))
    lam_ref, sub_ref = next(it), next(it)
    o_ref = next(it)
    if emit_kv:
        nk_ref, nv_ref = next(it), next(it)
    kb_ref, vb_ref = next(it), next(it)

    @pl.when(pl.program_id(2) == 0)
    def _():
        k = k_ref[0]
        v = v_ref[0]
        if emit_kv:
            nk_ref[0] = k
            nv_ref[0] = v
        for mp in range(2):
            km = k[:, mp * A_QK:(mp + 1) * A_QK]
            if use_ctx:
                km = _rope(km, ckk_ref[...], skk_ref[...])
            kb_ref[0:t_own, mp * A_QK:(mp + 1) * A_QK] = km.astype(BF16)
        vb_ref[0:t_own, :] = v.astype(BF16)
        if use_ctx:
            kb_ref[t_own:, :] = ck_ref[0].astype(BF16)
            vb_ref[t_own:, :] = cv_ref[0].astype(BF16)

    lp = lam_ref[...]
    lam = (jnp.exp(jnp.sum(lp[0:1] * lp[1:2], axis=-1, keepdims=True))
           - jnp.exp(jnp.sum(lp[2:3] * lp[3:4], axis=-1, keepdims=True)) + lam_init)
    q = q_ref[0]
    scale = (A_QK ** -0.5) * LOG2E
    es, ls = [], []
    for mp in range(2):
        qm = q[:, mp * A_QK:(mp + 1) * A_QK]
        if use_ctx:
            qm = _rope(qm, cq_ref[...], sq_ref[...])
        qm = (qm * scale).astype(BF16)
        s = lax.dot_general(qm, kb_ref[:, mp * A_QK:(mp + 1) * A_QK], (((1,), (1,)), ((), ())),
                            preferred_element_type=F32)
        e = jnp.exp2(s - jnp.max(s, axis=-1, keepdims=True))
        es.append(e)
        ls.append(jnp.sum(e, axis=-1, keepdims=True))
    w = es[0] * (1.0 / ls[0]) - es[1] * (lam / ls[1])
    o = jnp.dot(w.astype(BF16), vb_ref[...], preferred_element_type=F32)
    o = o * lax.rsqrt(jnp.mean(o * o, axis=-1, keepdims=True) + EPS) * sub_ref[...]
    o_ref[0] = (o * (1.0 - lam_init)).astype(o_ref.dtype)


def _attention(proj, lam_p, subln, lam_init, ctx=None, rope=None, tq=256):
    bsz, t, _ = proj.shape
    tq = min(tq, t)
    hw = 2 * A_QK
    use_ctx = ctx is not None
    args = [proj, proj, proj]
    specs = [pl.BlockSpec((1, tq, hw), lambda b, h, i: (b, i, h)),
             pl.BlockSpec((1, t, hw), lambda b, h, i: (b, 0, A_HEADS + h)),
             pl.BlockSpec((1, t, A_V), lambda b, h, i: (b, 0, 2 * A_HEADS + h))]
    t_all = t
    if use_ctx:
        ck, cv = ctx
        p = ck.shape[1]
        t_all = t + p
        cos, sin = rope
        args += [ck.reshape(bsz, p, A_HEADS * hw), cv.reshape(bsz, p, A_WIDTH), cos, sin, cos, sin]
        specs += [pl.BlockSpec((1, p, hw), lambda b, h, i: (b, 0, h)),
                  pl.BlockSpec((1, p, A_V), lambda b, h, i: (b, 0, h)),
                  pl.BlockSpec((tq, A_QK), lambda b, h, i: (i, 0)),
                  pl.BlockSpec((tq, A_QK), lambda b, h, i: (i, 0)),
                  pl.BlockSpec((t, A_QK), lambda b, h, i: (0, 0)),
                  pl.BlockSpec((t, A_QK), lambda b, h, i: (0, 0))]
    args += [lam_p, subln.reshape(1, A_V)]
    specs += [pl.BlockSpec((4, A_QK), lambda b, h, i: (0, 0)),
              pl.BlockSpec((1, A_V), lambda b, h, i: (0, 0))]
    out_shape = [jax.ShapeDtypeStruct((bsz, t, A_WIDTH), BF16)]
    out_specs = [pl.BlockSpec((1, tq, A_V), lambda b, h, i: (b, i, h))]
    if not use_ctx:
        out_shape += [jax.ShapeDtypeStruct((bsz, t, A_WIDTH), F32)] * 2
        out_specs += [pl.BlockSpec((1, t, hw), lambda b, h, i: (b, 0, h)),
                      pl.BlockSpec((1, t, A_V), lambda b, h, i: (b, 0, h))]
    kern = functools.partial(_attn_kernel, use_ctx=use_ctx, emit_kv=not use_ctx, lam_init=lam_init, t_own=t)
    return pl.pallas_call(
        kern, out_shape=out_shape, grid=(bsz, A_HEADS, t // tq), in_specs=specs, out_specs=out_specs,
        scratch_shapes=[pltpu.VMEM((t_all, hw), BF16), pltpu.VMEM((t_all, A_V), BF16)],
        compiler_params=_params(("arbitrary", "arbitrary", "arbitrary")), name="diff_attention",
    )(*args)


def _level_table(rev):
    i = np.arange(CHUNK)[:, None]
    j = np.arange(CHUNK)[None, :]
    x = i ^ j
    lvl = np.where(x == 0, 0, 2 ** np.floor(np.log2(np.maximum(x, 1))).astype(np.int64))
    keep = (i <= j) if rev else (i >= j)
    return jnp.asarray(np.where(keep, lvl, -1), dtype=jnp.int32)


def _scan_chunk(q, k, v, g, st, lvl, rev):
    c, dk = g.shape
    row = lax.broadcasted_iota(jnp.int32, (c, dk), 0)
    pos = (c - 1 - row) if rev else row

    def prev(x, d):
        return pltpu.roll(x, (c - d) if rev else d, axis=0)

    def nxt(x, d):
        return pltpu.roll(x, d if rev else (c - d), axis=0)

    b = g
    d = 1
    while d < c:
        b = b + jnp.where(pos >= d, prev(b, d), 0.0)
        d *= 2
    u1 = nxt(g, 1)
    u2 = u1 + nxt(g, 2)
    u3 = u2 + nxt(g, 3)
    l1 = g + prev(g, 1)
    l2 = l1 + prev(g, 2)
    l3 = l2 + prev(g, 3)
    p8 = pos % 8
    args = {
        1: jnp.where(pos % 2 == 1, g, 0.0),
        2: jnp.where(pos % 4 == 0, u1, jnp.where(pos % 4 == 1, 0.0, jnp.where(pos % 4 == 2, g, l1))),
        4: jnp.where(p8 == 0, u3, jnp.where(p8 == 1, u2, jnp.where(p8 == 2, u1, jnp.where(
            p8 == 3, 0.0, jnp.where(p8 == 4, g, jnp.where(p8 == 5, l1, jnp.where(p8 == 6, l2, l3))))))),
    }
    s = 8
    while s < c:
        parts = []
        for blk in range(c // (2 * s)):
            m = blk * 2 * s + (s if rev else s - 1)
            parts.append(jnp.broadcast_to(b[m:m + 1, :], (2 * s, dk)))
        bref = jnp.concatenate(parts, axis=0) if len(parts) > 1 else parts[0]
        args[s] = jnp.where(pos % (2 * s) >= s, b - bref, bref - b)
        s *= 2

    nt = (((1,), (1,)), ((), ()))
    a = jnp.where(lvl == 0, lax.dot_general(q.astype(BF16), k.astype(BF16), nt, preferred_element_type=F32), 0.0)
    for s, arg in args.items():
        e = jnp.exp2(arg)
        a_s = lax.dot_general((q * e).astype(BF16), (k * e).astype(BF16), nt, preferred_element_type=F32)
        a = jnp.where(lvl == s, a_s, a)

    last = 0 if rev else c - 1
    b_last = b[last:last + 1, :]
    o = lax.dot_general((q * jnp.exp2(b)).astype(BF16), st.astype(BF16), nt, preferred_element_type=F32)
    o = o + jnp.dot(a.astype(BF16), v.astype(BF16), preferred_element_type=F32)
    k_dec = (k * jnp.exp2(b_last - b)).astype(BF16)
    st_new = jnp.exp2(b_last) * st + lax.dot_general(v.astype(BF16), k_dec, (((0,), (0,)), ((), ())),
                                                     preferred_element_type=F32)
    return o, st_new


def _scan_kernel(*refs, mode, has_s0, emit_state, t, dk, dv):
    it = iter(refs)
    q_ref = next(it)
    if mode == "hgrn":
        zf_ref, zb_ref, v_ref, og_ref, lbf_ref, lbb_ref = (next(it) for _ in range(6))
    else:
        k_ref, v_ref, og_ref, lr_ref, w2f_ref, w2b_ref, gbf_ref, gbb_ref = (next(it) for _ in range(8))
    if has_s0:
        s0f_ref, s0b_ref = next(it), next(it)
    onorm_ref, lvf_ref, lvb_ref = next(it), next(it), next(it)
    o_ref = next(it)
    if emit_state:
        sf_ref, sb_ref = next(it), next(it)
    oacc_ref, stf_ref, stb_ref = next(it), next(it), next(it)
    n = t // CHUNK
    half = n // 2

    def load(ci, rev):
        r0 = pl.multiple_of(ci * CHUNK, CHUNK)
        rows = pl.ds(r0, CHUNK)
        q = q_ref[0, rows, :]
        v = v_ref[0, rows, :]
        if mode == "hgrn":
            z = (zb_ref if rev else zf_ref)[0, rows, :]
            lb = (lbb_ref if rev else lbf_ref)[...]
            f = lb + (1.0 - lb) * jax.nn.sigmoid(z)
            k = 1.0 - f
            g = jnp.log2(f)
        else:
            k = k_ref[0, rows, :]
            q = q * (dk ** -0.5)
            x = jnp.dot(lr_ref[0, rows, :].astype(BF16), (w2b_ref if rev else w2f_ref)[...],
                        preferred_element_type=F32) + (gbb_ref if rev else gbf_ref)[...]
            g = jax.nn.log_sigmoid(x) * (LOG2E / C_GATE_NORM)
        return rows, q, k, v, g

    if has_s0:
        stf_ref[...] = s0f_ref[0, 0, 0].T
        stb_ref[...] = s0b_ref[0, 0, 0].T
    else:
        stf_ref[...] = jnp.zeros_like(stf_ref)
        stb_ref[...] = jnp.zeros_like(stb_ref)

    unroll = 2 if half % 2 == 0 else 1

    def body(it, carry, final):
        for u, rev in [(u, rev) for u in range(unroll) for rev in (False, True)]:
            step = it * unroll + u
            ci = (n - 1 - step) if rev else step
            st_ref = stb_ref if rev else stf_ref
            rows, q, k, v, g = load(ci, rev)
            o, st_new = _scan_chunk(q, k, v, g, st_ref[...], (lvb_ref if rev else lvf_ref)[...], rev)
            st_ref[...] = st_new
            if not final:
                oacc_ref[rows, :] = o
            else:
                o = o + oacc_ref[rows, :]
                o = o * lax.rsqrt(jnp.mean(o * o, axis=-1, keepdims=True) + EPS) * onorm_ref[...]
                og = og_ref[0, rows, :]
                o_ref[0, rows, :] = (o * (og * jax.nn.sigmoid(og))).astype(o_ref.dtype)
        return carry

    lax.fori_loop(0, half // unroll, functools.partial(body, final=False), 0)
    lax.fori_loop(half // unroll, n // unroll, functools.partial(body, final=True), 0)
    if emit_state:
        sf_ref[0, 0] = stf_ref[...].T
        sb_ref[0, 0] = stb_ref[...].T


def _scan(mode, proj, col0, onorm, *, heads, dk, dv, extra, s0=None, emit_state=False):
    bsz, t, _ = proj.shape

    def col(name, width):
        base = col0[name] // width
        return pl.BlockSpec((1, t, width), lambda b, h: (b, 0, base + h))

    args, specs = [proj], [col("q", dk)]
    if mode == "hgrn":
        lbf, lbb = extra
        args += [proj, proj, proj, proj, lbf, lbb]
        specs += [col("zf", dk), col("zb", dk), col("v", dv), col("og", dv),
                  pl.BlockSpec((1, dk), lambda b, h: (0, h)), pl.BlockSpec((1, dk), lambda b, h: (0, h))]
    else:
        lr, w2f, w2b, gbf, gbb = extra
        args += [proj, proj, proj, lr, w2f, w2b, gbf, gbb]
        specs += [col("k", dk), col("v", dv), col("og", dv),
                  pl.BlockSpec((1, t, LANES), lambda b, h: (b, 0, 0)),
                  pl.BlockSpec((LANES, dk), lambda b, h: (0, h)), pl.BlockSpec((LANES, dk), lambda b, h: (0, h)),
                  pl.BlockSpec((1, dk), lambda b, h: (0, h)), pl.BlockSpec((1, dk), lambda b, h: (0, h))]
    st_block = (1, 1, 1, dk, dv)
    if s0 is not None:
        args += [s0, s0]
        specs += [pl.BlockSpec(st_block, lambda b, h: (b, 0, h, 0, 0)),
                  pl.BlockSpec(st_block, lambda b, h: (b, 1, h, 0, 0))]
    args += [onorm.reshape(1, dv), _level_table(False), _level_table(True)]
    specs += [pl.BlockSpec((1, dv), lambda b, h: (0, 0)),
              pl.BlockSpec((CHUNK, CHUNK), lambda b, h: (0, 0)),
              pl.BlockSpec((CHUNK, CHUNK), lambda b, h: (0, 0))]
    out_shape = [jax.ShapeDtypeStruct((bsz, t, heads * dv), BF16)]
    out_specs = [pl.BlockSpec((1, t, dv), lambda b, h: (b, 0, h))]
    if emit_state:
        out_shape += [jax.ShapeDtypeStruct((bsz, heads, dk, dv), F32)] * 2
        out_specs += [pl.BlockSpec((1, 1, dk, dv), lambda b, h: (b, h, 0, 0))] * 2
    kern = functools.partial(_scan_kernel, mode=mode, has_s0=s0 is not None, emit_state=emit_state,
                             t=t, dk=dk, dv=dv)
    return pl.pallas_call(
        kern, out_shape=out_shape, grid=(bsz, heads), in_specs=specs, out_specs=out_specs,
        scratch_shapes=[pltpu.VMEM((t, dv), F32), pltpu.VMEM((dv, dk), F32), pltpu.VMEM((dv, dk), F32)],
        compiler_params=_params(("arbitrary", "arbitrary")), name="gated_scan_" + mode,
    )(*args)


DMA_GROUP = 8


def _moe_kernel(idx_ref, h_hbm, gate_ref, wg_ref, wu_ref, wd_ref, zero_hbm, out_hbm,
                xrow_ref, orow_ref, xb_ref, hid_ref, sem_x, sem_o, *, tm, nf, tf, nd, tn):
    del zero_hbm
    s = pl.program_id(2)
    tile = pl.program_id(0) * pl.num_programs(1) + pl.program_id(1)
    n_tiles = pl.num_programs(0) * pl.num_programs(1)
    base = tile * tm

    def x_copy(b0, r):
        tok = idx_ref[b0 + r]
        return pltpu.make_async_copy(h_hbm.at[pl.ds(tok, 1)], xrow_ref.at[pl.ds(r, 1)], sem_x)

    def o_gather(r):
        tok = idx_ref[base + r]
        return pltpu.make_async_copy(out_hbm.at[pl.ds(tok, 1)], orow_ref.at[pl.ds(r, 1)], sem_o)

    def o_scatter(r):
        tok = idx_ref[base + r]
        return pltpu.make_async_copy(orow_ref.at[pl.ds(r, 1)], out_hbm.at[pl.ds(tok, 1)], sem_o)

    def for_rows(fn):
        def body(i, c):
            for u in range(DMA_GROUP):
                fn(i * DMA_GROUP + u)
            return c
        lax.fori_loop(0, tm // DMA_GROUP, body, 0)

    @pl.when(s == 0)
    def _():
        @pl.when(tile == 0)
        def _():
            for_rows(lambda r: x_copy(base, r).start())
        for_rows(lambda r: x_copy(base, r).wait())
        xb_ref[...] = xrow_ref[...].astype(BF16)

        @pl.when(tile + 1 < n_tiles)
        def _():
            for_rows(lambda r: x_copy(base + tm, r).start())

    @pl.when(s == 1)
    def _():
        @pl.when(tile > 0)
        def _():
            for_rows(lambda r: o_scatter(r).wait())
        for_rows(lambda r: o_gather(r).start())

    @pl.when(s < nf)
    def _():
        x = xb_ref[...]
        g = jnp.dot(x, wg_ref[0, 0], preferred_element_type=F32)
        u = jnp.dot(x, wu_ref[0, 0], preferred_element_type=F32)
        hid_ref[s] = (g * jax.nn.sigmoid(g) * u).astype(BF16)

    for n in range(nd):
        @pl.when(s == nf + n)
        def _(n=n):
            if n == 0:
                for_rows(lambda r: o_gather(r).wait())
            acc = jnp.dot(hid_ref[0], wd_ref[0, 0, 0:tf, :], preferred_element_type=F32)
            for f in range(1, nf):
                acc = acc + jnp.dot(hid_ref[f], wd_ref[0, 0, f * tf:(f + 1) * tf, :], preferred_element_type=F32)
            orow_ref[:, n * tn:(n + 1) * tn] += acc * gate_ref[0]
            if n == nd - 1:
                for_rows(lambda r: o_scatter(r).start())

                @pl.when(tile == n_tiles - 1)
                def _():
                    for_rows(lambda r: o_scatter(r).wait())


def _moe(h, idx, gate, wg, wu, wd, li, tm=512, tf=256, tn=1024):
    n, d = h.shape
    n_e, cap = idx.shape
    dexp = wg.shape[3]
    tm, tf, tn = min(tm, cap), min(tf, dexp), min(tn, d)
    nf, nd = dexp // tf, d // tn
    assert nf >= 2 and tm % DMA_GROUP == 0
    grid_spec = pltpu.PrefetchScalarGridSpec(
        num_scalar_prefetch=1,
        grid=(n_e, cap // tm, nf + nd),
        in_specs=[pl.BlockSpec(memory_space=pl.ANY),
                  pl.BlockSpec((1, tm, 1), lambda e, m, s, idx: (e, m, 0)),
                  pl.BlockSpec((1, 1, d, tf), lambda e, m, s, idx: (li, e, 0, jnp.minimum(s, nf - 1))),
                  pl.BlockSpec((1, 1, d, tf), lambda e, m, s, idx: (li, e, 0, jnp.minimum(s, nf - 1))),
                  pl.BlockSpec((1, 1, dexp, tn), lambda e, m, s, idx: (li, e, 0, jnp.maximum(s - nf, 0))),
                  pl.BlockSpec(memory_space=pl.ANY)],
        out_specs=pl.BlockSpec(memory_space=pl.ANY),
        scratch_shapes=[pltpu.VMEM((tm, d), F32), pltpu.VMEM((tm, d), F32), pltpu.VMEM((tm, d), BF16),
                        pltpu.VMEM((nf, tm, tf), BF16), pltpu.SemaphoreType.DMA, pltpu.SemaphoreType.DMA],
    )
    return pl.pallas_call(
        functools.partial(_moe_kernel, tm=tm, nf=nf, tf=tf, nd=nd, tn=tn),
        out_shape=jax.ShapeDtypeStruct((n, d), F32), grid_spec=grid_spec,
        input_output_aliases={6: 0},
        compiler_params=_params(("arbitrary", "arbitrary", "arbitrary")), name="moe_ffn",
    )(idx.reshape(-1), h, gate.reshape(n_e, cap, 1), wg, wu, wd, jnp.zeros((n, d), F32))


def _mods(ada, lo, hi):
    d = ada.shape[1] // 6
    return [ada[lo:hi, k * d:(k + 1) * d].reshape(hi - lo, 1, d) for k in range(6)]


def _ab_mixer(h, w_in, lam_p, subln, lb, onorm, lam_init, rope=None, ctx=None):
    bsz, t, d = h.shape
    proj = _mm([h.reshape(bsz * t, d)], w_in).reshape(bsz, t, -1)
    bw = B_HEADS * B_KEY
    col0 = {"q": 3 * A_WIDTH, "zf": 3 * A_WIDTH + bw, "zb": 3 * A_WIDTH + 2 * bw,
            "v": 3 * A_WIDTH + 3 * bw, "og": 3 * A_WIDTH + 4 * bw}
    extra = (lb[0:1], lb[1:2])
    if ctx is None:
        o_a, k_c, v_c = _attention(proj, lam_p, subln, lam_init)
        o_b, s_f, s_b = _scan("hgrn", proj, col0, onorm, heads=B_HEADS, dk=B_KEY, dv=B_VAL, extra=extra,
                              emit_state=True)
        return (o_a, o_b), (k_c, v_c, jnp.stack([s_f, s_b], axis=1))
    ck, cv, cs = ctx
    (o_a,) = _attention(proj, lam_p, subln, lam_init, ctx=(ck, cv), rope=rope)
    (o_b,) = _scan("hgrn", proj, col0, onorm, heads=B_HEADS, dk=B_KEY, dv=B_VAL, extra=extra, s0=cs)
    return (o_a, o_b), None


def _gla_mixer(h, w_main, w_lr, w2f, w2b, gbf, gbb, onorm, ctx_state=None):
    bsz, t, d = h.shape
    h2d = h.reshape(bsz * t, d)
    n_main = C_HEADS * (2 * C_KEY + 2 * C_VAL)
    proj = _mm([h2d], w_main, n_cols=n_main).reshape(bsz, t, n_main)
    lr = _mm([h2d], w_lr).reshape(bsz, t, LANES)
    col0 = {"q": 0, "k": C_HEADS * C_KEY, "v": 2 * C_HEADS * C_KEY, "og": 2 * C_HEADS * C_KEY + C_HEADS * C_VAL}
    extra = (lr, w2f, w2b, gbf, gbb)
    if ctx_state is None:
        o, s_f, s_b = _scan("gla", proj, col0, onorm, heads=C_HEADS, dk=C_KEY, dv=C_VAL, extra=extra,
                            emit_state=True)
        return (o,), jnp.stack([s_f, s_b], axis=1)
    (o,) = _scan("gla", proj, col0, onorm, heads=C_HEADS, dk=C_KEY, dv=C_VAL, extra=extra, s0=ctx_state)
    return (o,), None


def kernel(x_prompt, x_sample, cache_k_a, cache_v_a, state_hgrn, state_gla, c, c_ctx, ada_w, ada_b, norm_mix, norm_ffn, ab_w_in, ab_lambda, ab_subln, ab_lb_logits, ab_onorm, ab_w_out, c_w_in, c_gate_w2, c_gate_b, c_onorm, c_w_out, router_w, exp_w_gate, exp_w_up, exp_w_down, final_norm):
    d = x_prompt.shape[-1]
    n_s = c.shape[0]
    lb_table = jnp.cumsum(jax.nn.softmax(ab_lb_logits.astype(F32), axis=0), axis=0)
    rope = _rope_tables(x_sample.shape[1])
    cond_rows = 8 * ((1 + n_s + 7) // 8)
    cond = jnp.concatenate([c_ctx[None, :], c, jnp.zeros((cond_rows - 1 - n_s, d), F32)], axis=0)

    ys = [x_prompt, x_sample]
    pending = [None, None]
    new_k = new_v = new_hgrn = new_gla = None
    wg, wu, wd = exp_w_gate.astype(BF16), exp_w_up.astype(BF16), exp_w_down.astype(BF16)
    for li in range(DEPTH):
        ada = _adaln(cond, ada_w, ada_b, li)
        mods = [_mods(ada, 0, 1), _mods(ada, 1, 1 + n_s)]
        j = li // 2
        if li % 2 == 0:
            lam_init = 0.8 - 0.6 * math.exp(-0.3 * li)
            w_in = ab_w_in[j].astype(BF16)
            w_out = ab_w_out[j].astype(BF16)
        else:
            n_main = C_HEADS * (2 * C_KEY + 2 * C_VAL)
            w_in = c_w_in[j].astype(BF16)
            w_lr = jnp.pad(c_w_in[j][:, n_main:], ((0, 0), (0, LANES - 2 * C_RANK))).astype(BF16)
            w2 = c_gate_w2[j]
            w2f = jnp.pad(w2[0], ((0, LANES - C_RANK), (0, 0))).astype(BF16)
            w2b = jnp.pad(w2[1], ((C_RANK, LANES - 2 * C_RANK), (0, 0))).astype(BF16)
            gbf, gbb = c_gate_b[j][0:1], c_gate_b[j][1:2]
            w_out = c_w_out[j].astype(BF16)
        for gi in range(2):
            sh1, sc1, g1, sh2, sc2, g2 = mods[gi]
            y = ys[gi]
            bsz, t, _ = y.shape
            if pending[gi] is None:
                (h,) = _norm(y, norm_mix[li], mod=(sc1, sh1), out_dtype=BF16)
            else:
                y, h = _norm(y, norm_mix[li], resid=pending[gi], mod=(sc1, sh1), emit_y=True, out_dtype=BF16)
            is_ctx = gi == 0
            if li % 2 == 0:
                ctx = None if is_ctx else (cache_k_a[:, j], cache_v_a[:, j], state_hgrn[:, j])
                outs, cache = _ab_mixer(h, w_in, ab_lambda[j], ab_subln[j], lb_table[li], ab_onorm[j], lam_init,
                                        rope=rope, ctx=ctx)
                if is_ctx:
                    new_k, new_v, new_hgrn = cache
            else:
                outs, cache = _gla_mixer(h, w_in, w_lr, w2f, w2b, gbf, gbb, c_onorm[j],
                                         ctx_state=None if is_ctx else state_gla[:, j])
                if is_ctx:
                    new_gla = cache
            xs = [o.reshape(bsz * t, -1) for o in outs]
            y1 = _mm(xs, w_out, resid=(y.reshape(bsz * t, d), g1), rows_per_cond=t).reshape(bsz, t, d)
            h2, aff = _norm(y1, norm_ffn[li], mod=(sc2, sh2), router_w=router_w[li], out_dtype=F32)
            n_tok = bsz * t
            n_e = router_w.shape[-1]
            cap = EC_FACTOR * n_tok // n_e
            gate, idx = lax.top_k(aff.reshape(n_tok, ROUTER_PAD)[:, :n_e].T, cap)
            moe = _moe(h2.reshape(n_tok, d), idx, gate, wg, wu, wd, li).reshape(bsz, t, d)
            ys[gi] = y1
            pending[gi] = (moe, g2)
    outs = []
    for gi in range(2):
        (yo,) = _norm(ys[gi], final_norm, resid=pending[gi], out_dtype=F32)
        outs.append(yo)
    bp, tp = x_prompt.shape[:2]
    return (outs[0], outs[1],
            new_k.reshape(bp, tp, A_HEADS, 2 * A_QK)[:, None],
            new_v.reshape(bp, tp, A_HEADS, A_V)[:, None],
            new_hgrn[:, None], new_gla[:, None])
```

```python
import functools
import math

import numpy as np
import jax
import jax.numpy as jnp
from jax import lax
from jax.experimental import pallas as pl
from jax.experimental.pallas import tpu as pltpu

D_MODEL = 4096
DEPTH = 2
GRID_W = 64
A_HEADS = 8
A_V = 256
A_QK = 128
A_WIDTH = A_HEADS * A_V
B_KEY = 128
B_VAL = 128
B_WIDTH = D_MODEL // 2
B_HEADS = B_WIDTH // B_KEY
C_HEADS = 8
C_KEY = D_MODEL // 2 // C_HEADS
C_VAL = D_MODEL // C_HEADS
C_RANK = 16
C_GATE_NORM = 16.0
N_EXPERTS = 16
EC_FACTOR = 2
D_EXPERT = D_MODEL // 2
CHUNK = 64
ROPE_BASE = 10000.0
EPS = 1e-6
LOG2E = math.log2(math.e)
F32 = jnp.float32
BF16 = jnp.bfloat16

LANES = 128
VMEM_LIMIT = 56 * 1024 * 1024
ROUTER_PAD = LANES


def _params(sem, vmem=VMEM_LIMIT):
    return pltpu.CompilerParams(dimension_semantics=sem, vmem_limit_bytes=vmem)


def _adaln_kernel(c_ref, w_ref, b_ref, o_ref):
    c = c_ref[...]
    a = (c * jax.nn.sigmoid(c)).astype(BF16)
    o_ref[...] = jnp.dot(a, w_ref[...].astype(BF16), preferred_element_type=F32) + b_ref[...]


def _adaln(cond, w, b, li):
    rows, d = cond.shape
    n_layers, _, n = w.shape
    tn = 512
    return pl.pallas_call(
        _adaln_kernel,
        out_shape=jax.ShapeDtypeStruct((rows, n), F32),
        grid=(n // tn,),
        in_specs=[pl.BlockSpec((rows, d), lambda j: (0, 0)),
                  pl.BlockSpec((None, d, tn), lambda j: (li, 0, j)),
                  pl.BlockSpec((None, 1, tn), lambda j: (li, 0, j))],
        out_specs=pl.BlockSpec((rows, tn), lambda j: (0, j)),
        compiler_params=_params(("arbitrary",)),
        name="adaln",
    )(cond, w, b.reshape(n_layers, 1, n))


def _norm_kernel(*refs, has_resid, has_mod, has_router, emit_y, n_experts):
    it = iter(refs)
    y_ref = next(it)
    if has_resid:
        m_ref, gate_ref = next(it), next(it)
    g_ref = next(it)
    if has_mod:
        sc_ref, sh_ref = next(it), next(it)
    if has_router:
        rw_ref = next(it)
    if emit_y:
        yo_ref = next(it)
    h_ref = next(it)
    if has_router:
        aff_ref = next(it)

    x = y_ref[0]
    if has_resid:
        x = x + gate_ref[0] * m_ref[0]
    if emit_y:
        yo_ref[0] = x
    ms = jnp.mean(x * x, axis=-1, keepdims=True)
    h = (x * lax.rsqrt(ms + EPS)) * g_ref[...]
    if has_mod:
        h = h * (1.0 + sc_ref[0]) + sh_ref[0]
    h_ref[0] = h.astype(h_ref.dtype)
    if has_router:
        logits = jnp.dot(h, rw_ref[...], precision=lax.Precision.HIGHEST, preferred_element_type=F32)
        col = lax.broadcasted_iota(jnp.int32, logits.shape, 1)
        logits = jnp.where(col < n_experts, logits, -jnp.inf)
        e = jnp.exp(logits - jnp.max(logits, axis=-1, keepdims=True))
        aff_ref[0] = e / jnp.sum(e, axis=-1, keepdims=True)


def _norm(y, gnorm, *, resid=None, mod=None, router_w=None, emit_y=False, out_dtype=F32, tt=256):
    bsz, t, d = y.shape
    tt = min(tt, t)
    tok = pl.BlockSpec((1, tt, d), lambda b, i: (b, i, 0))

    def cond_spec(a):
        if a.shape[0] == 1:
            return pl.BlockSpec((1, 1, d), lambda b, i: (0, 0, 0))
        return pl.BlockSpec((1, 1, d), lambda b, i: (b, 0, 0))

    args, specs = [y], [tok]
    if resid is not None:
        args += [resid[0], resid[1]]
        specs += [tok, cond_spec(resid[1])]
    args.append(gnorm.reshape(1, d))
    specs.append(pl.BlockSpec((1, d), lambda b, i: (0, 0)))
    if mod is not None:
        args += [mod[0], mod[1]]
        specs += [cond_spec(mod[0]), cond_spec(mod[1])]
    if router_w is not None:
        rw = jnp.pad(router_w, ((0, 0), (0, ROUTER_PAD - router_w.shape[1])))
        args.append(rw)
        specs.append(pl.BlockSpec((d, ROUTER_PAD), lambda b, i: (0, 0)))
    out_shape, out_specs = [], []
    if emit_y:
        out_shape.append(jax.ShapeDtypeStruct((bsz, t, d), F32))
        out_specs.append(tok)
    out_shape.append(jax.ShapeDtypeStruct((bsz, t, d), out_dtype))
    out_specs.append(tok)
    if router_w is not None:
        out_shape.append(jax.ShapeDtypeStruct((bsz, t, ROUTER_PAD), F32))
        out_specs.append(pl.BlockSpec((1, tt, ROUTER_PAD), lambda b, i: (b, i, 0)))
    kern = functools.partial(_norm_kernel, has_resid=resid is not None, has_mod=mod is not None,
                             has_router=router_w is not None, emit_y=emit_y,
                             n_experts=0 if router_w is None else router_w.shape[1])
    return pl.pallas_call(
        kern, out_shape=out_shape, grid=(bsz, t // tt), in_specs=specs, out_specs=out_specs,
        compiler_params=_params(("arbitrary", "arbitrary")), name="norm",
    )(*args)


def _mm_kernel(*refs, n_in, has_resid):
    xs = refs[:n_in]
    ws = refs[n_in:2 * n_in]
    rest = refs[2 * n_in:]
    acc = jnp.dot(xs[0][...], ws[0][...], preferred_element_type=F32)
    for x_ref, w_ref in zip(xs[1:], ws[1:]):
        acc = acc + jnp.dot(x_ref[...], w_ref[...], preferred_element_type=F32)
    if has_resid:
        y_ref, gate_ref, o_ref = rest
        o_ref[...] = y_ref[...] + gate_ref[0] * acc
    else:
        (o_ref,) = rest
        o_ref[...] = acc.astype(o_ref.dtype)


def _mm(xs, w, *, n_cols=None, resid=None, rows_per_cond=None, out_dtype=F32, tm=1024, tn=1024):
    m = xs[0].shape[0]
    n = w.shape[1] if n_cols is None else n_cols
    tm, tn = min(tm, m), min(tn, n)
    if resid is not None and resid[1].shape[0] > 1:
        tm = min(tm, rows_per_cond)
    specs, args = [], []
    for x in xs:
        args.append(x)
        specs.append(pl.BlockSpec((tm, x.shape[1]), lambda i, j: (i, 0)))
    kb = 0
    for x in xs:
        k = x.shape[1]
        args.append(w)
        specs.append(pl.BlockSpec((k, tn), functools.partial(lambda i, j, r: (r, j), r=kb // k)))
        kb += k
    if resid is not None:
        y, gate = resid
        args += [y, gate]
        specs.append(pl.BlockSpec((tm, tn), lambda i, j: (i, j)))
        if gate.shape[0] == 1:
            specs.append(pl.BlockSpec((1, 1, tn), lambda i, j: (0, 0, j)))
        else:
            per = rows_per_cond // tm
            specs.append(pl.BlockSpec((1, 1, tn), lambda i, j: (i // per, 0, j)))
    kern = functools.partial(_mm_kernel, n_in=len(xs), has_resid=resid is not None)
    return pl.pallas_call(
        kern, out_shape=jax.ShapeDtypeStruct((m, n), out_dtype), grid=(m // tm, n // tn),
        in_specs=specs, out_specs=pl.BlockSpec((tm, tn), lambda i, j: (i, j)),
        compiler_params=_params(("arbitrary", "arbitrary")), name="matmul",
    )(*args)


def _rope_tables(n_tokens):
    rows = n_tokens // GRID_W
    row = jnp.repeat(jnp.arange(rows, dtype=F32), GRID_W)
    col = jnp.tile(jnp.arange(GRID_W, dtype=F32), rows)
    axis_dim = A_QK // 2
    inv = ROPE_BASE ** (-jnp.arange(0, axis_dim, 2, dtype=F32) / axis_dim)
    ar, ac = row[:, None] * inv, col[:, None] * inv
    cos = jnp.concatenate([jnp.cos(ar), jnp.cos(ar), jnp.cos(ac), jnp.cos(ac)], axis=-1)
    sin = jnp.concatenate([-jnp.sin(ar), jnp.sin(ar), -jnp.sin(ac), jnp.sin(ac)], axis=-1)
    return cos, sin


def _rope(x, cos, sin):
    q = A_QK // 4
    lane = lax.broadcasted_iota(jnp.int32, x.shape, 1)
    first = (lane % (2 * q)) < q
    partner = jnp.where(first, pltpu.roll(x, A_QK - q, axis=1), pltpu.roll(x, q, axis=1))
    return x * cos + partner * sin


def _attn_kernel(*refs, use_ctx, emit_kv, lam_init, t_own):
    it = iter(refs)
    q_ref, k_ref, v_ref = next(it), next(it), next(it)
    if use_ctx:
        ck_ref, cv_ref, cq_ref, sq_ref, ckk_ref, skk_ref = (next(it) for _ in range(6))
    lam_ref, sub_ref = next(it), next(it)
    o_ref = next(it)
    if emit_kv:
        nk_ref, nv_ref = next(it), next(it)
    kb_ref, vb_ref = next(it), next(it)

    @pl.when(pl.program_id(2) == 0)
    def _():
        k = k_ref[0]
        v = v_ref[0]
        if emit_kv:
            nk_ref[0] = k
            nv_ref[0] = v
        for mp in range(2):
            km = k[:, mp * A_QK:(mp + 1) * A_QK]
            if use_ctx:
                km = _rope(km, ckk_ref[...], skk_ref[...])
            kb_ref[0:t_own, mp * A_QK:(mp + 1) * A_QK] = km.astype(BF16)
        vb_ref[0:t_own, :] = v.astype(BF16)
        if use_ctx:
            kb_ref[t_own:, :] = ck_ref[0].astype(BF16)
            vb_ref[t_own:, :] = cv_ref[0].astype(BF16)

    lp = lam_ref[...]
    lam = (jnp.exp(jnp.sum(lp[0:1] * lp[1:2], axis=-1, keepdims=True))
           - jnp.exp(jnp.sum(lp[2:3] * lp[3:4], axis=-1, keepdims=True)) + lam_init)
    q = q_ref[0]
    scale = (A_QK ** -0.5) * LOG2E
    es, ls = [], []
    for mp in range(2):
        qm = q[:, mp * A_QK:(mp + 1) * A_QK]
        if use_ctx:
            qm = _rope(qm, cq_ref[...], sq_ref[...])
        qm = (qm * scale).astype(BF16)
        s = lax.dot_general(qm, kb_ref[:, mp * A_QK:(mp + 1) * A_QK], (((1,), (1,)), ((), ())),
                            preferred_element_type=F32)
        e = jnp.exp2(s - jnp.max(s, axis=-1, keepdims=True))
        es.append(e)
        ls.append(jnp.sum(e, axis=-1, keepdims=True))
    w = es[0] * (1.0 / ls[0]) - es[1] * (lam / ls[1])
    o = jnp.dot(w.astype(BF16), vb_ref[...], preferred_element_type=F32)
    o = o * lax.rsqrt(jnp.mean(o * o, axis=-1, keepdims=True) + EPS) * sub_ref[...]
    o_ref[0] = (o * (1.0 - lam_init)).astype(o_ref.dtype)


def _attention(proj, lam_p, subln, lam_init, ctx=None, rope=None, tq=256):
    bsz, t, _ = proj.shape
    tq = min(tq, t)
    hw = 2 * A_QK
    use_ctx = ctx is not None
    args = [proj, proj, proj]
    specs = [pl.BlockSpec((1, tq, hw), lambda b, h, i: (b, i, h)),
             pl.BlockSpec((1, t, hw), lambda b, h, i: (b, 0, A_HEADS + h)),
             pl.BlockSpec((1, t, A_V), lambda b, h, i: (b, 0, 2 * A_HEADS + h))]
    t_all = t
    if use_ctx:
        ck, cv = ctx
        p = ck.shape[1]
        t_all = t + p
        cos, sin = rope
        args += [ck.reshape(bsz, p, A_HEADS * hw), cv.reshape(bsz, p, A_WIDTH), cos, sin, cos, sin]
        specs += [pl.BlockSpec((1, p, hw), lambda b, h, i: (b, 0, h)),
                  pl.BlockSpec((1, p, A_V), lambda b, h, i: (b, 0, h)),
                  pl.BlockSpec((tq, A_QK), lambda b, h, i: (i, 0)),
                  pl.BlockSpec((tq, A_QK), lambda b, h, i: (i, 0)),
                  pl.BlockSpec((t, A_QK), lambda b, h, i: (0, 0)),
                  pl.BlockSpec((t, A_QK), lambda b, h, i: (0, 0))]
    args += [lam_p, subln.reshape(1, A_V)]
    specs += [pl.BlockSpec((4, A_QK), lambda b, h, i: (0, 0)),
              pl.BlockSpec((1, A_V), lambda b, h, i: (0, 0))]
    out_shape = [jax.ShapeDtypeStruct((bsz, t, A_WIDTH), BF16)]
    out_specs = [pl.BlockSpec((1, tq, A_V), lambda b, h, i: (b, i, h))]
    if not use_ctx:
        out_shape += [jax.ShapeDtypeStruct((bsz, t, A_WIDTH), F32)] * 2
        out_specs += [pl.BlockSpec((1, t, hw), lambda b, h, i: (b, 0, h)),
                      pl.BlockSpec((1, t, A_V), lambda b, h, i: (b, 0, h))]
    kern = functools.partial(_attn_kernel, use_ctx=use_ctx, emit_kv=not use_ctx, lam_init=lam_init, t_own=t)
    return pl.pallas_call(
        kern, out_shape=out_shape, grid=(bsz, A_HEADS, t // tq), in_specs=specs, out_specs=out_specs,
        scratch_shapes=[pltpu.VMEM((t_all, hw), BF16), pltpu.VMEM((t_all, A_V), BF16)],
        compiler_params=_params(("arbitrary", "arbitrary", "arbitrary")), name="diff_attention",
    )(*args)


def _level_table(rev):
    i = np.arange(CHUNK)[:, None]
    j = np.arange(CHUNK)[None, :]
    x = i ^ j
    lvl = np.where(x == 0, 0, 2 ** np.floor(np.log2(np.maximum(x, 1))).astype(np.int64))
    keep = (i <= j) if rev else (i >= j)
    return jnp.asarray(np.where(keep, lvl, -1), dtype=jnp.int32)


def _chunk_local(q, k, g, lvl, rev):
    c, dk = g.shape
    row = lax.broadcasted_iota(jnp.int32, (c, dk), 0)
    pos = (c - 1 - row) if rev else row

    def prev(x, d):
        return pltpu.roll(x, (c - d) if rev else d, axis=0)

    sub = 8
    ng = c // sub
    p8 = pos % sub

    pre = g
    for d in (1, 2, 4):
        pre = pre + jnp.where(p8 >= d, prev(pre, d), 0.0)
    groups = [pre[sub * i:sub * (i + 1)] for i in range(ng)]
    tot_row = 0 if rev else sub - 1
    off = None
    b_groups = [None] * ng
    for i in (reversed(range(ng)) if rev else range(ng)):
        b_groups[i] = groups[i] if off is None else groups[i] + off
        tot = groups[i][tot_row:tot_row + 1, :]
        off = tot if off is None else off + tot
    b = jnp.concatenate(b_groups, axis=0)

    def group_row(p):
        a = (sub - 1 - p) if rev else p
        return jnp.concatenate([jnp.broadcast_to(gr[a:a + 1, :], (sub, dk)) for gr in groups], axis=0)

    ref2 = jnp.where(p8 < 4, group_row(1), group_row(5))
    ref4 = group_row(3)
    args = {
        1: jnp.where(p8 % 2 == 1, g, 0.0),
        2: jnp.where(p8 % 4 >= 2, pre - ref2, ref2 - pre),
        4: jnp.where(p8 >= 4, pre - ref4, ref4 - pre),
    }
    s = 8
    while s < c:
        parts = []
        for blk in range(c // (2 * s)):
            m = blk * 2 * s + (s if rev else s - 1)
            parts.append(jnp.broadcast_to(b[m:m + 1, :], (2 * s, dk)))
        bref = jnp.concatenate(parts, axis=0) if len(parts) > 1 else parts[0]
        args[s] = jnp.where(pos % (2 * s) >= s, b - bref, bref - b)
        s *= 2

    nt = (((1,), (1,)), ((), ()))
    a = jnp.where(lvl == 0, lax.dot_general(q.astype(BF16), k.astype(BF16), nt, preferred_element_type=F32), 0.0)
    for s, arg in args.items():
        e = jnp.exp2(arg)
        a_s = lax.dot_general((q * e).astype(BF16), (k * e).astype(BF16), nt, preferred_element_type=F32)
        a = jnp.where(lvl == s, a_s, a)

    last = 0 if rev else c - 1
    b_last = b[last:last + 1, :]
    q_dec = (q * jnp.exp2(b)).astype(BF16)
    k_dec = (k * jnp.exp2(b_last - b)).astype(BF16)
    return a.astype(BF16), q_dec, k_dec, jnp.exp2(b_last)


def _chunk_state(local, v, st):
    a, q_dec, k_dec, decay = local
    vb = v.astype(BF16)
    o = lax.dot_general(q_dec, st.astype(BF16), (((1,), (1,)), ((), ())), preferred_element_type=F32)
    o = o + jnp.dot(a, vb, preferred_element_type=F32)
    st_new = decay * st + lax.dot_general(vb, k_dec, (((0,), (0,)), ((), ())), preferred_element_type=F32)
    return o, st_new


def _scan_kernel(*refs, mode, has_s0, emit_state, t, dk, dv):
    it = iter(refs)
    q_ref = next(it)
    if mode == "hgrn":
        zf_ref, zb_ref, v_ref, og_ref, lbf_ref, lbb_ref = (next(it) for _ in range(6))
    else:
        k_ref, v_ref, og_ref, lr_ref, w2f_ref, w2b_ref, gbf_ref, gbb_ref = (next(it) for _ in range(8))
    if has_s0:
        s0f_ref, s0b_ref = next(it), next(it)
    onorm_ref, lvf_ref, lvb_ref = next(it), next(it), next(it)
    o_ref = next(it)
    if emit_state:
        sf_ref, sb_ref = next(it), next(it)
    oacc_ref, stf_ref, stb_ref = next(it), next(it), next(it)
    n = t // CHUNK
    half = n // 2

    def chunk_rows(step, rev):
        ci = (n - 1 - step) if rev else step
        return pl.ds(pl.multiple_of(ci * CHUNK, CHUNK), CHUNK)

    def local(step, rev):
        rows = chunk_rows(step, rev)
        q = q_ref[0, rows, :]
        if mode == "hgrn":
            z = (zb_ref if rev else zf_ref)[0, rows, :]
            lb = (lbb_ref if rev else lbf_ref)[...]
            f = lb + (1.0 - lb) * jax.nn.sigmoid(z)
            k = 1.0 - f
            g = jnp.log2(f)
        else:
            k = k_ref[0, rows, :]
            q = q * (dk ** -0.5)
            x = jnp.dot(lr_ref[0, rows, :].astype(BF16), (w2b_ref if rev else w2f_ref)[...],
                        preferred_element_type=F32) + (gbb_ref if rev else gbf_ref)[...]
            g = jax.nn.log_sigmoid(x) * (LOG2E / C_GATE_NORM)
        return _chunk_local(q, k, g, (lvb_ref if rev else lvf_ref)[...], rev)

    if has_s0:
        stf_ref[...] = s0f_ref[0, 0, 0].T
        stb_ref[...] = s0b_ref[0, 0, 0].T
    else:
        stf_ref[...] = jnp.zeros_like(stf_ref)
        stb_ref[...] = jnp.zeros_like(stb_ref)

    unroll = 4 if half % 4 == 0 else (2 if half % 2 == 0 else 1)
    trips, half_trips = n // unroll, half // unroll
    lanes = [(u, rev) for u in range(unroll) for rev in (False, True)]

    def local_stage(trip):
        return tuple(local(trip * unroll + u, rev) for u, rev in lanes)

    def state_stage(trip, locals_, final):
        for (u, rev), loc in zip(lanes, locals_):
            rows = chunk_rows(trip * unroll + u, rev)
            st_ref = stb_ref if rev else stf_ref
            o, st_new = _chunk_state(loc, v_ref[0, rows, :], st_ref[...])
            st_ref[...] = st_new
            if not final:
                oacc_ref[rows, :] = o
            else:
                o = o + oacc_ref[rows, :]
                o = o * lax.rsqrt(jnp.mean(o * o, axis=-1, keepdims=True) + EPS) * onorm_ref[...]
                og = og_ref[0, rows, :]
                o_ref[0, rows, :] = (o * (og * jax.nn.sigmoid(og))).astype(o_ref.dtype)

    def body(trip, carry, final):
        nxt = local_stage(trip + 1)
        state_stage(trip, carry, final)
        return nxt

    carry = local_stage(0)
    carry = lax.fori_loop(0, half_trips, functools.partial(body, final=False), carry)
    carry = lax.fori_loop(half_trips, trips - 1, functools.partial(body, final=True), carry)
    state_stage(trips - 1, carry, True)
    if emit_state:
        sf_ref[0, 0] = stf_ref[...].T
        sb_ref[0, 0] = stb_ref[...].T


def _scan(mode, proj, col0, onorm, *, heads, dk, dv, extra, s0=None, emit_state=False):
    bsz, t, _ = proj.shape

    def col(name, width):
        base = col0[name] // width
        return pl.BlockSpec((1, t, width), lambda b, h: (b, 0, base + h))

    args, specs = [proj], [col("q", dk)]
    if mode == "hgrn":
        lbf, lbb = extra
        args += [proj, proj, proj, proj, lbf, lbb]
        specs += [col("zf", dk), col("zb", dk), col("v", dv), col("og", dv),
                  pl.BlockSpec((1, dk), lambda b, h: (0, h)), pl.BlockSpec((1, dk), lambda b, h: (0, h))]
    else:
        lr, w2f, w2b, gbf, gbb = extra
        args += [proj, proj, proj, lr, w2f, w2b, gbf, gbb]
        specs += [col("k", dk), col("v", dv), col("og", dv),
                  pl.BlockSpec((1, t, LANES), lambda b, h: (b, 0, 0)),
                  pl.BlockSpec((LANES, dk), lambda b, h: (0, h)), pl.BlockSpec((LANES, dk), lambda b, h: (0, h)),
                  pl.BlockSpec((1, dk), lambda b, h: (0, h)), pl.BlockSpec((1, dk), lambda b, h: (0, h))]
    st_block = (1, 1, 1, dk, dv)
    if s0 is not None:
        args += [s0, s0]
        specs += [pl.BlockSpec(st_block, lambda b, h: (b, 0, h, 0, 0)),
                  pl.BlockSpec(st_block, lambda b, h: (b, 1, h, 0, 0))]
    args += [onorm.reshape(1, dv), _level_table(False), _level_table(True)]
    specs += [pl.BlockSpec((1, dv), lambda b, h: (0, 0)),
              pl.BlockSpec((CHUNK, CHUNK), lambda b, h: (0, 0)),
              pl.BlockSpec((CHUNK, CHUNK), lambda b, h: (0, 0))]
    out_shape = [jax.ShapeDtypeStruct((bsz, t, heads * dv), BF16)]
    out_specs = [pl.BlockSpec((1, t, dv), lambda b, h: (b, 0, h))]
    if emit_state:
        out_shape += [jax.ShapeDtypeStruct((bsz, heads, dk, dv), F32)] * 2
        out_specs += [pl.BlockSpec((1, 1, dk, dv), lambda b, h: (b, h, 0, 0))] * 2
    kern = functools.partial(_scan_kernel, mode=mode, has_s0=s0 is not None, emit_state=emit_state,
                             t=t, dk=dk, dv=dv)
    return pl.pallas_call(
        kern, out_shape=out_shape, grid=(bsz, heads), in_specs=specs, out_specs=out_specs,
        scratch_shapes=[pltpu.VMEM((t, dv), F32), pltpu.VMEM((dv, dk), F32), pltpu.VMEM((dv, dk), F32)],
        compiler_params=_params(("arbitrary", "arbitrary")), name="gated_scan_" + mode,
    )(*args)


DMA_GROUP = 8


def _moe_kernel(idx_ref, h_hbm, gate_ref, wg_ref, wu_ref, wd_ref, zero_hbm, out_hbm,
                xrow_ref, orow_ref, xb_ref, hid_ref, sem_x, sem_o, *, tm, nf, tf, nd, tn):
    del zero_hbm
    s = pl.program_id(2)
    tile = pl.program_id(0) * pl.num_programs(1) + pl.program_id(1)
    n_tiles = pl.num_programs(0) * pl.num_programs(1)
    base = tile * tm
    next_base = jnp.minimum(base + tm, (n_tiles - 1) * tm)
    prev_base = jnp.maximum(base - tm, 0)
    half = nf // 2
    rows_x, rows_o = tm // nf, tm // half

    def x_copy(b0, grp, u):
        tok = idx_ref[b0 + grp * rows_x + u]
        return pltpu.make_async_copy(h_hbm.at[pl.ds(tok, 1)], xrow_ref.at[grp, pl.ds(u, 1)], sem_x)

    def o_gather(grp, u):
        tok = idx_ref[base + grp * rows_o + u]
        return pltpu.make_async_copy(out_hbm.at[pl.ds(tok, 1)], orow_ref.at[grp, pl.ds(u, 1)], sem_o)

    def o_scatter(b0, grp, u):
        tok = idx_ref[b0 + grp * rows_o + u]
        return pltpu.make_async_copy(orow_ref.at[grp, pl.ds(u, 1)], out_hbm.at[pl.ds(tok, 1)], sem_o)

    def for_rows(fn, rows):
        per = rows // DMA_GROUP
        def body(i, c):
            for u in range(DMA_GROUP):
                fn(i // per, (i % per) * DMA_GROUP + u)
            return c
        lax.fori_loop(0, tm // DMA_GROUP, body, 0)

    @pl.when(s == 0)
    def _():
        @pl.when(tile == 0)
        def _():
            orow_ref[...] = jnp.zeros_like(orow_ref)
            for_rows(lambda g, u: x_copy(base, g, u).start(), rows_x)
        for_rows(lambda g, u: x_copy(base, g, u).wait(), rows_x)
        xb_ref[...] = xrow_ref[...].reshape(tm, xb_ref.shape[1]).astype(BF16)

    def up_step(first_half):
        for u in range(rows_x):
            x_copy(next_base, s, u).start()
        for u in range(rows_o):
            if first_half:
                o_scatter(prev_base, s, u).start()
            else:
                o_gather(s - half, u).start()
        x = xb_ref[...]
        g = jnp.dot(x, wg_ref[0, 0], preferred_element_type=F32)
        u = jnp.dot(x, wu_ref[0, 0], preferred_element_type=F32)
        hid_ref[s] = (g * jax.nn.sigmoid(g) * u).astype(BF16)

    @pl.when(s < half)
    def _():
        up_step(True)

    @pl.when(s == half)
    def _():
        for_rows(lambda g, u: o_scatter(prev_base, g, u).wait(), rows_o)

    @pl.when((s >= half) & (s < nf))
    def _():
        up_step(False)

    for n in range(nd):
        @pl.when(s == nf + n)
        def _(n=n):
            if n == 0:
                for_rows(lambda g, u: o_gather(g, u).wait(), rows_o)
            acc = jnp.dot(hid_ref[0], wd_ref[0, 0, 0:tf, :], preferred_element_type=F32)
            for f in range(1, nf):
                acc = acc + jnp.dot(hid_ref[f], wd_ref[0, 0, f * tf:(f + 1) * tf, :], preferred_element_type=F32)
            orow_ref[:, :, n * tn:(n + 1) * tn] += (acc * gate_ref[0]).reshape(half, rows_o, tn)
            if n == nd - 1:
                @pl.when(tile == n_tiles - 1)
                def _():
                    for_rows(lambda g, u: o_scatter(base, g, u).start(), rows_o)
                    for_rows(lambda g, u: o_scatter(base, g, u).wait(), rows_o)
                    for_rows(lambda g, u: x_copy(base, g, u).wait(), rows_x)


def _moe(h, idx, gate, wg, wu, wd, li, tm=512, tf=256, tn=1024):
    n, d = h.shape
    n_e, cap = idx.shape
    dexp = wg.shape[3]
    tm, tf, tn = min(tm, cap), min(tf, dexp), min(tn, d)
    nf, nd = dexp // tf, d // tn
    assert nf % 2 == 0 and tm % nf == 0 and tm % DMA_GROUP == 0
    grid_spec = pltpu.PrefetchScalarGridSpec(
        num_scalar_prefetch=1,
        grid=(n_e, cap // tm, nf + nd),
        in_specs=[pl.BlockSpec(memory_space=pl.ANY),
                  pl.BlockSpec((1, tm, 1), lambda e, m, s, idx: (e, m, 0)),
                  pl.BlockSpec((1, 1, d, tf), lambda e, m, s, idx: (li, e, 0, jnp.minimum(s, nf - 1))),
                  pl.BlockSpec((1, 1, d, tf), lambda e, m, s, idx: (li, e, 0, jnp.minimum(s, nf - 1))),
                  pl.BlockSpec((1, 1, dexp, tn), lambda e, m, s, idx: (li, e, 0, jnp.maximum(s - nf, 0))),
                  pl.BlockSpec(memory_space=pl.ANY)],
        out_specs=pl.BlockSpec(memory_space=pl.ANY),
        scratch_shapes=[pltpu.VMEM((nf, tm // nf, d), F32), pltpu.VMEM((nf // 2, 2 * tm // nf, d), F32),
                        pltpu.VMEM((tm, d), BF16),
                        pltpu.VMEM((nf, tm, tf), BF16), pltpu.SemaphoreType.DMA, pltpu.SemaphoreType.DMA],
    )
    return pl.pallas_call(
        functools.partial(_moe_kernel, tm=tm, nf=nf, tf=tf, nd=nd, tn=tn),
        out_shape=jax.ShapeDtypeStruct((n, d), F32), grid_spec=grid_spec,
        input_output_aliases={6: 0},
        compiler_params=_params(("arbitrary", "arbitrary", "arbitrary")), name="moe_ffn",
    )(idx.reshape(-1), h, gate.reshape(n_e, cap, 1), wg, wu, wd, jnp.zeros((n, d), F32))


def _mods(ada, lo, hi):
    d = ada.shape[1] // 6
    return [ada[lo:hi, k * d:(k + 1) * d].reshape(hi - lo, 1, d) for k in range(6)]


def _ab_mixer(h, w_in, lam_p, subln, lb, onorm, lam_init, rope=None, ctx=None):
    bsz, t, d = h.shape
    proj = _mm([h.reshape(bsz * t, d)], w_in).reshape(bsz, t, -1)
    bw = B_HEADS * B_KEY
    col0 = {"q": 3 * A_WIDTH, "zf": 3 * A_WIDTH + bw, "zb": 3 * A_WIDTH + 2 * bw,
            "v": 3 * A_WIDTH + 3 * bw, "og": 3 * A_WIDTH + 4 * bw}
    extra = (lb[0:1], lb[1:2])
    if ctx is None:
        o_a, k_c, v_c = _attention(proj, lam_p, subln, lam_init)
        o_b, s_f, s_b = _scan("hgrn", proj, col0, onorm, heads=B_HEADS, dk=B_KEY, dv=B_VAL, extra=extra,
                              emit_state=True)
        return (o_a, o_b), (k_c, v_c, jnp.stack([s_f, s_b], axis=1))
    ck, cv, cs = ctx
    (o_a,) = _attention(proj, lam_p, subln, lam_init, ctx=(ck, cv), rope=rope)
    (o_b,) = _scan("hgrn", proj, col0, onorm, heads=B_HEADS, dk=B_KEY, dv=B_VAL, extra=extra, s0=cs)
    return (o_a, o_b), None


def _gla_mixer(h, w_main, w_lr, w2f, w2b, gbf, gbb, onorm, ctx_state=None):
    bsz, t, d = h.shape
    h2d = h.reshape(bsz * t, d)
    n_main = C_HEADS * (2 * C_KEY + 2 * C_VAL)
    proj = _mm([h2d], w_main, n_cols=n_main).reshape(bsz, t, n_main)
    lr = _mm([h2d], w_lr).reshape(bsz, t, LANES)
    col0 = {"q": 0, "k": C_HEADS * C_KEY, "v": 2 * C_HEADS * C_KEY, "og": 2 * C_HEADS * C_KEY + C_HEADS * C_VAL}
    extra = (lr, w2f, w2b, gbf, gbb)
    if ctx_state is None:
        o, s_f, s_b = _scan("gla", proj, col0, onorm, heads=C_HEADS, dk=C_KEY, dv=C_VAL, extra=extra,
                            emit_state=True)
        return (o,), jnp.stack([s_f, s_b], axis=1)
    (o,) = _scan("gla", proj, col0, onorm, heads=C_HEADS, dk=C_KEY, dv=C_VAL, extra=extra, s0=ctx_state)
    return (o,), None


def kernel(x_prompt, x_sample, cache_k_a, cache_v_a, state_hgrn, state_gla, c, c_ctx, ada_w, ada_b, norm_mix, norm_ffn, ab_w_in, ab_lambda, ab_subln, ab_lb_logits, ab_onorm, ab_w_out, c_w_in, c_gate_w2, c_gate_b, c_onorm, c_w_out, router_w, exp_w_gate, exp_w_up, exp_w_down, final_norm):
    d = x_prompt.shape[-1]
    n_s = c.shape[0]
    lb_table = jnp.cumsum(jax.nn.softmax(ab_lb_logits.astype(F32), axis=0), axis=0)
    rope = _rope_tables(x_sample.shape[1])
    cond_rows = 8 * ((1 + n_s + 7) // 8)
    cond = jnp.concatenate([c_ctx[None, :], c, jnp.zeros((cond_rows - 1 - n_s, d), F32)], axis=0)

    ys = [x_prompt, x_sample]
    pending = [None, None]
    new_k = new_v = new_hgrn = new_gla = None
    wg, wu, wd = exp_w_gate.astype(BF16), exp_w_up.astype(BF16), exp_w_down.astype(BF16)
    for li in range(DEPTH):
        ada = _adaln(cond, ada_w, ada_b, li)
        mods = [_mods(ada, 0, 1), _mods(ada, 1, 1 + n_s)]
        j = li // 2
        if li % 2 == 0:
            lam_init = 0.8 - 0.6 * math.exp(-0.3 * li)
            w_in = ab_w_in[j].astype(BF16)
            w_out = ab_w_out[j].astype(BF16)
        else:
            n_main = C_HEADS * (2 * C_KEY + 2 * C_VAL)
            w_in = c_w_in[j].astype(BF16)
            w_lr = jnp.pad(c_w_in[j][:, n_main:], ((0, 0), (0, LANES - 2 * C_RANK))).astype(BF16)
            w2 = c_gate_w2[j]
            w2f = jnp.pad(w2[0], ((0, LANES - C_RANK), (0, 0))).astype(BF16)
            w2b = jnp.pad(w2[1], ((C_RANK, LANES - 2 * C_RANK), (0, 0))).astype(BF16)
            gbf, gbb = c_gate_b[j][0:1], c_gate_b[j][1:2]
            w_out = c_w_out[j].astype(BF16)
        for gi in range(2):
            sh1, sc1, g1, sh2, sc2, g2 = mods[gi]
            y = ys[gi]
            bsz, t, _ = y.shape
            if pending[gi] is None:
                (h,) = _norm(y, norm_mix[li], mod=(sc1, sh1), out_dtype=BF16)
            else:
                y, h = _norm(y, norm_mix[li], resid=pending[gi], mod=(sc1, sh1), emit_y=True, out_dtype=BF16)
            is_ctx = gi == 0
            if li % 2 == 0:
                ctx = None if is_ctx else (cache_k_a[:, j], cache_v_a[:, j], state_hgrn[:, j])
                outs, cache = _ab_mixer(h, w_in, ab_lambda[j], ab_subln[j], lb_table[li], ab_onorm[j], lam_init,
                                        rope=rope, ctx=ctx)
                if is_ctx:
                    new_k, new_v, new_hgrn = cache
            else:
                outs, cache = _gla_mixer(h, w_in, w_lr, w2f, w2b, gbf, gbb, c_onorm[j],
                                         ctx_state=None if is_ctx else state_gla[:, j])
                if is_ctx:
                    new_gla = cache
            xs = [o.reshape(bsz * t, -1) for o in outs]
            y1 = _mm(xs, w_out, resid=(y.reshape(bsz * t, d), g1), rows_per_cond=t).reshape(bsz, t, d)
            h2, aff = _norm(y1, norm_ffn[li], mod=(sc2, sh2), router_w=router_w[li], out_dtype=F32)
            n_tok = bsz * t
            n_e = router_w.shape[-1]
            cap = EC_FACTOR * n_tok // n_e
            gate, idx = lax.top_k(aff.reshape(n_tok, ROUTER_PAD)[:, :n_e].T, cap)
            moe = _moe(h2.reshape(n_tok, d), idx, gate, wg, wu, wd, li).reshape(bsz, t, d)
            ys[gi] = y1
            pending[gi] = (moe, g2)
    outs = []
    for gi in range(2):
        (yo,) = _norm(ys[gi], final_norm, resid=pending[gi], out_dtype=F32)
        outs.append(yo)
    bp, tp = x_prompt.shape[:2]
    return (outs[0], outs[1],
            new_k.reshape(bp, tp, A_HEADS, 2 * A_QK)[:, None],
            new_v.reshape(bp, tp, A_HEADS, A_V)[:, None],
            new_hgrn[:, None], new_gla[:, None])
```

```python
import functools
import math

import numpy as np
import jax
import jax.numpy as jnp
from jax import lax
from jax.experimental import pallas as pl
from jax.experimental.pallas import tpu as pltpu

D_MODEL = 4096
DEPTH = 2
GRID_W = 64
A_HEADS = 8
A_V = 256
A_QK = 128
A_WIDTH = A_HEADS * A_V
B_KEY = 128
B_VAL = 128
B_WIDTH = D_MODEL // 2
B_HEADS = B_WIDTH // B_KEY
C_HEADS = 8
C_KEY = D_MODEL // 2 // C_HEADS
C_VAL = D_MODEL // C_HEADS
C_RANK = 16
C_GATE_NORM = 16.0
N_EXPERTS = 16
EC_FACTOR = 2
D_EXPERT = D_MODEL // 2
CHUNK = 64
ROPE_BASE = 10000.0
EPS = 1e-6
LOG2E = math.log2(math.e)
F32 = jnp.float32
BF16 = jnp.bfloat16

LANES = 128
MXU_DIM = 256
VMEM_LIMIT = 56 * 1024 * 1024
ROUTER_PAD = LANES


def _params(sem, vmem=VMEM_LIMIT):
    return pltpu.CompilerParams(dimension_semantics=sem, vmem_limit_bytes=vmem)


def _adaln_kernel(c_ref, w_ref, b_ref, o_ref):
    c = c_ref[...]
    a = (c * jax.nn.sigmoid(c)).astype(BF16)
    o_ref[...] = jnp.dot(a, w_ref[...].astype(BF16), preferred_element_type=F32) + b_ref[...]


def _adaln(cond, w, b, li):
    rows, d = cond.shape
    n_layers, _, n = w.shape
    tn = 512
    return pl.pallas_call(
        _adaln_kernel,
        out_shape=jax.ShapeDtypeStruct((rows, n), F32),
        grid=(n // tn,),
        in_specs=[pl.BlockSpec((rows, d), lambda j: (0, 0)),
                  pl.BlockSpec((None, d, tn), lambda j: (li, 0, j)),
                  pl.BlockSpec((None, 1, tn), lambda j: (li, 0, j))],
        out_specs=pl.BlockSpec((rows, tn), lambda j: (0, j)),
        compiler_params=_params(("arbitrary",)),
        name="adaln",
    )(cond, w, b.reshape(n_layers, 1, n))


def _norm_kernel(*refs, has_resid, has_mod, has_router, emit_y, n_experts):
    it = iter(refs)
    y_ref = next(it)
    if has_resid:
        m_ref, gate_ref = next(it), next(it)
    g_ref = next(it)
    if has_mod:
        sc_ref, sh_ref = next(it), next(it)
    if has_router:
        rw_ref = next(it)
    if emit_y:
        yo_ref = next(it)
    h_ref = next(it)
    if has_router:
        aff_ref = next(it)

    x = y_ref[0]
    if has_resid:
        x = x + gate_ref[0] * m_ref[0]
    if emit_y:
        yo_ref[0] = x
    ms = jnp.mean(x * x, axis=-1, keepdims=True)
    h = (x * lax.rsqrt(ms + EPS)) * g_ref[...]
    if has_mod:
        h = h * (1.0 + sc_ref[0]) + sh_ref[0]
    h_ref[0] = h.astype(h_ref.dtype)
    if has_router:
        logits = jnp.dot(h, rw_ref[...], precision=lax.Precision.HIGHEST, preferred_element_type=F32)
        col = lax.broadcasted_iota(jnp.int32, logits.shape, 1)
        logits = jnp.where(col < n_experts, logits, -jnp.inf)
        e = jnp.exp(logits - jnp.max(logits, axis=-1, keepdims=True))
        aff_ref[0] = e / jnp.sum(e, axis=-1, keepdims=True)


def _norm(y, gnorm, *, resid=None, mod=None, router_w=None, emit_y=False, out_dtype=F32, tt=256):
    bsz, t, d = y.shape
    tt = min(tt, t)
    tok = pl.BlockSpec((1, tt, d), lambda b, i: (b, i, 0))

    def cond_spec(a):
        if a.shape[0] == 1:
            return pl.BlockSpec((1, 1, d), lambda b, i: (0, 0, 0))
        return pl.BlockSpec((1, 1, d), lambda b, i: (b, 0, 0))

    args, specs = [y], [tok]
    if resid is not None:
        args += [resid[0], resid[1]]
        specs += [tok, cond_spec(resid[1])]
    args.append(gnorm.reshape(1, d))
    specs.append(pl.BlockSpec((1, d), lambda b, i: (0, 0)))
    if mod is not None:
        args += [mod[0], mod[1]]
        specs += [cond_spec(mod[0]), cond_spec(mod[1])]
    if router_w is not None:
        rw = jnp.pad(router_w, ((0, 0), (0, ROUTER_PAD - router_w.shape[1])))
        args.append(rw)
        specs.append(pl.BlockSpec((d, ROUTER_PAD), lambda b, i: (0, 0)))
    out_shape, out_specs = [], []
    if emit_y:
        out_shape.append(jax.ShapeDtypeStruct((bsz, t, d), F32))
        out_specs.append(tok)
    out_shape.append(jax.ShapeDtypeStruct((bsz, t, d), out_dtype))
    out_specs.append(tok)
    if router_w is not None:
        out_shape.append(jax.ShapeDtypeStruct((bsz, t, ROUTER_PAD), F32))
        out_specs.append(pl.BlockSpec((1, tt, ROUTER_PAD), lambda b, i: (b, i, 0)))
    kern = functools.partial(_norm_kernel, has_resid=resid is not None, has_mod=mod is not None,
                             has_router=router_w is not None, emit_y=emit_y,
                             n_experts=0 if router_w is None else router_w.shape[1])
    return pl.pallas_call(
        kern, out_shape=out_shape, grid=(bsz, t // tt), in_specs=specs, out_specs=out_specs,
        compiler_params=_params(("arbitrary", "arbitrary")), name="norm",
    )(*args)


def _mm_kernel(*refs, n_in, has_resid):
    xs = refs[:n_in]
    ws = refs[n_in:2 * n_in]
    rest = refs[2 * n_in:]
    acc = jnp.dot(xs[0][...], ws[0][...], preferred_element_type=F32)
    for x_ref, w_ref in zip(xs[1:], ws[1:]):
        acc = acc + jnp.dot(x_ref[...], w_ref[...], preferred_element_type=F32)
    if has_resid:
        y_ref, gate_ref, o_ref = rest
        o_ref[...] = y_ref[...] + gate_ref[0] * acc
    else:
        (o_ref,) = rest
        o_ref[...] = acc.astype(o_ref.dtype)


def _mm(xs, w, *, n_cols=None, resid=None, rows_per_cond=None, out_dtype=F32, tm=1024, tn=1024):
    m = xs[0].shape[0]
    n = w.shape[1] if n_cols is None else n_cols
    tm, tn = min(tm, m), min(tn, n)
    if resid is not None and resid[1].shape[0] > 1:
        tm = min(tm, rows_per_cond)
    specs, args = [], []
    for x in xs:
        args.append(x)
        specs.append(pl.BlockSpec((tm, x.shape[1]), lambda i, j: (i, 0)))
    kb = 0
    for x in xs:
        k = x.shape[1]
        args.append(w)
        specs.append(pl.BlockSpec((k, tn), functools.partial(lambda i, j, r: (r, j), r=kb // k)))
        kb += k
    if resid is not None:
        y, gate = resid
        args += [y, gate]
        specs.append(pl.BlockSpec((tm, tn), lambda i, j: (i, j)))
        if gate.shape[0] == 1:
            specs.append(pl.BlockSpec((1, 1, tn), lambda i, j: (0, 0, j)))
        else:
            per = rows_per_cond // tm
            specs.append(pl.BlockSpec((1, 1, tn), lambda i, j: (i // per, 0, j)))
    kern = functools.partial(_mm_kernel, n_in=len(xs), has_resid=resid is not None)
    return pl.pallas_call(
        kern, out_shape=jax.ShapeDtypeStruct((m, n), out_dtype), grid=(m // tm, n // tn),
        in_specs=specs, out_specs=pl.BlockSpec((tm, tn), lambda i, j: (i, j)),
        compiler_params=_params(("arbitrary", "arbitrary")), name="matmul",
    )(*args)


def _rope_tables(n_tokens):
    rows = n_tokens // GRID_W
    row = jnp.repeat(jnp.arange(rows, dtype=F32), GRID_W)
    col = jnp.tile(jnp.arange(GRID_W, dtype=F32), rows)
    axis_dim = A_QK // 2
    inv = ROPE_BASE ** (-jnp.arange(0, axis_dim, 2, dtype=F32) / axis_dim)
    ar, ac = row[:, None] * inv, col[:, None] * inv
    cos = jnp.concatenate([jnp.cos(ar), jnp.cos(ar), jnp.cos(ac), jnp.cos(ac)], axis=-1)
    sin = jnp.concatenate([-jnp.sin(ar), jnp.sin(ar), -jnp.sin(ac), jnp.sin(ac)], axis=-1)
    return cos, sin


def _rope(x, cos, sin):
    q = A_QK // 4
    lane = lax.broadcasted_iota(jnp.int32, x.shape, 1)
    first = (lane % (2 * q)) < q
    partner = jnp.where(first, pltpu.roll(x, A_QK - q, axis=1), pltpu.roll(x, q, axis=1))
    return x * cos + partner * sin


def _attn_kernel(*refs, use_ctx, emit_kv, lam_init, t_own):
    it = iter(refs)
    q_ref, k_ref, v_ref = next(it), next(it), next(it)
    if use_ctx:
        ck_ref, cv_ref, cq_ref, sq_ref, ckk_ref, skk_ref = (next(it) for _ in range(6))
    lam_ref, sub_ref = next(it), next(it)
    o_ref = next(it)
    if emit_kv:
        nk_ref, nv_ref = next(it), next(it)
    kb_ref, vb_ref = next(it), next(it)

    @pl.when(pl.program_id(2) == 0)
    def _():
        k = k_ref[0]
        v = v_ref[0]
        if emit_kv:
            nk_ref[0] = k
            nv_ref[0] = v
        for mp in range(2):
            km = k[:, mp * A_QK:(mp + 1) * A_QK]
            if use_ctx:
                km = _rope(km, ckk_ref[...], skk_ref[...])
            kb_ref[0:t_own, mp * A_QK:(mp + 1) * A_QK] = km.astype(BF16)
        vb_ref[0:t_own, :] = v.astype(BF16)
        if use_ctx:
            kb_ref[t_own:, :] = ck_ref[0].astype(BF16)
            vb_ref[t_own:, :] = cv_ref[0].astype(BF16)

    lp = lam_ref[...]
    lam = (jnp.exp(jnp.sum(lp[0:1] * lp[1:2], axis=-1, keepdims=True))
           - jnp.exp(jnp.sum(lp[2:3] * lp[3:4], axis=-1, keepdims=True)) + lam_init)
    q = q_ref[0]
    scale = (A_QK ** -0.5) * LOG2E
    es, ls = [], []
    for mp in range(2):
        qm = q[:, mp * A_QK:(mp + 1) * A_QK]
        if use_ctx:
            qm = _rope(qm, cq_ref[...], sq_ref[...])
        qm = (qm * scale).astype(BF16)
        s = lax.dot_general(qm, kb_ref[:, mp * A_QK:(mp + 1) * A_QK], (((1,), (1,)), ((), ())),
                            preferred_element_type=F32)
        e = jnp.exp2(s - jnp.max(s, axis=-1, keepdims=True))
        es.append(e)
        ls.append(jnp.sum(e, axis=-1, keepdims=True))
    w = es[0] * (1.0 / ls[0]) - es[1] * (lam / ls[1])
    o = jnp.dot(w.astype(BF16), vb_ref[...], preferred_element_type=F32)
    o = o * lax.rsqrt(jnp.mean(o * o, axis=-1, keepdims=True) + EPS) * sub_ref[...]
    o_ref[0] = (o * (1.0 - lam_init)).astype(o_ref.dtype)


def _attention(proj, lam_p, subln, lam_init, ctx=None, rope=None, tq=256):
    bsz, t, _ = proj.shape
    tq = min(tq, t)
    hw = 2 * A_QK
    use_ctx = ctx is not None
    args = [proj, proj, proj]
    specs = [pl.BlockSpec((1, tq, hw), lambda b, h, i: (b, i, h)),
             pl.BlockSpec((1, t, hw), lambda b, h, i: (b, 0, A_HEADS + h)),
             pl.BlockSpec((1, t, A_V), lambda b, h, i: (b, 0, 2 * A_HEADS + h))]
    t_all = t
    if use_ctx:
        ck, cv = ctx
        p = ck.shape[1]
        t_all = t + p
        cos, sin = rope
        args += [ck.reshape(bsz, p, A_HEADS * hw), cv.reshape(bsz, p, A_WIDTH), cos, sin, cos, sin]
        specs += [pl.BlockSpec((1, p, hw), lambda b, h, i: (b, 0, h)),
                  pl.BlockSpec((1, p, A_V), lambda b, h, i: (b, 0, h)),
                  pl.BlockSpec((tq, A_QK), lambda b, h, i: (i, 0)),
                  pl.BlockSpec((tq, A_QK), lambda b, h, i: (i, 0)),
                  pl.BlockSpec((t, A_QK), lambda b, h, i: (0, 0)),
                  pl.BlockSpec((t, A_QK), lambda b, h, i: (0, 0))]
    args += [lam_p, subln.reshape(1, A_V)]
    specs += [pl.BlockSpec((4, A_QK), lambda b, h, i: (0, 0)),
              pl.BlockSpec((1, A_V), lambda b, h, i: (0, 0))]
    out_shape = [jax.ShapeDtypeStruct((bsz, t, A_WIDTH), BF16)]
    out_specs = [pl.BlockSpec((1, tq, A_V), lambda b, h, i: (b, i, h))]
    if not use_ctx:
        out_shape += [jax.ShapeDtypeStruct((bsz, t, A_WIDTH), F32)] * 2
        out_specs += [pl.BlockSpec((1, t, hw), lambda b, h, i: (b, 0, h)),
                      pl.BlockSpec((1, t, A_V), lambda b, h, i: (b, 0, h))]
    kern = functools.partial(_attn_kernel, use_ctx=use_ctx, emit_kv=not use_ctx, lam_init=lam_init, t_own=t)
    return pl.pallas_call(
        kern, out_shape=out_shape, grid=(bsz, A_HEADS, t // tq), in_specs=specs, out_specs=out_specs,
        scratch_shapes=[pltpu.VMEM((t_all, hw), BF16), pltpu.VMEM((t_all, A_V), BF16)],
        compiler_params=_params(("arbitrary", "arbitrary", "arbitrary")), name="diff_attention",
    )(*args)


def _level_table(rev):
    i = np.arange(CHUNK)[:, None]
    j = np.arange(CHUNK)[None, :]
    x = i ^ j
    lvl = np.where(x == 0, 0, 2 ** np.floor(np.log2(np.maximum(x, 1))).astype(np.int64))
    keep = (i <= j) if rev else (i >= j)
    return jnp.asarray(np.where(keep, lvl, -1), dtype=jnp.int32)


def _chunk_local(q, k, g, lvl, rev):
    c, dk = g.shape
    row = lax.broadcasted_iota(jnp.int32, (c, dk), 0)
    pos = (c - 1 - row) if rev else row

    def prev(x, d):
        return pltpu.roll(x, (c - d) if rev else d, axis=0)

    sub = 8
    ng = c // sub
    p8 = pos % sub

    pre = g
    for d in (1, 2, 4):
        pre = pre + jnp.where(p8 >= d, prev(pre, d), 0.0)
    groups = [pre[sub * i:sub * (i + 1)] for i in range(ng)]
    tot_row = 0 if rev else sub - 1
    off = None
    b_groups = [None] * ng
    for i in (reversed(range(ng)) if rev else range(ng)):
        b_groups[i] = groups[i] if off is None else groups[i] + off
        tot = groups[i][tot_row:tot_row + 1, :]
        off = tot if off is None else off + tot
    b = jnp.concatenate(b_groups, axis=0)

    def group_row(p):
        a = (sub - 1 - p) if rev else p
        return jnp.concatenate([jnp.broadcast_to(gr[a:a + 1, :], (sub, dk)) for gr in groups], axis=0)

    ref2 = jnp.where(p8 < 4, group_row(1), group_row(5))
    ref4 = group_row(3)
    args = {
        1: jnp.where(p8 % 2 == 1, g, 0.0),
        2: jnp.where(p8 % 4 >= 2, pre - ref2, ref2 - pre),
        4: jnp.where(p8 >= 4, pre - ref4, ref4 - pre),
    }
    s = 8
    while s < c:
        parts = []
        for blk in range(c // (2 * s)):
            m = blk * 2 * s + (s if rev else s - 1)
            parts.append(jnp.broadcast_to(b[m:m + 1, :], (2 * s, dk)))
        bref = jnp.concatenate(parts, axis=0) if len(parts) > 1 else parts[0]
        args[s] = jnp.where(pos % (2 * s) >= s, b - bref, bref - b)
        s *= 2

    nt = (((1,), (1,)), ((), ()))
    a = jnp.where(lvl == 0, lax.dot_general(q.astype(BF16), k.astype(BF16), nt, preferred_element_type=F32), 0.0)
    for s, arg in args.items():
        e = jnp.exp2(arg)
        a_s = lax.dot_general((q * e).astype(BF16), (k * e).astype(BF16), nt, preferred_element_type=F32)
        a = jnp.where(lvl == s, a_s, a)

    last = 0 if rev else c - 1
    b_last = b[last:last + 1, :]
    q_dec = (q * jnp.exp2(b)).astype(BF16)
    k_dec = (k * jnp.exp2(b_last - b)).astype(BF16)
    return a.astype(BF16), q_dec, k_dec, jnp.exp2(b_last)


def _chunk_state(local, v, st):
    a, q_dec, k_dec, decay = local
    vb = v.astype(BF16)
    o = lax.dot_general(q_dec, st.astype(BF16), (((1,), (1,)), ((), ())), preferred_element_type=F32)
    o = o + jnp.dot(a, vb, preferred_element_type=F32)
    st_new = decay * st + lax.dot_general(vb, k_dec, (((0,), (0,)), ((), ())), preferred_element_type=F32)
    return o, st_new


def _scan_kernel(*refs, mode, has_s0, emit_state, t, dk, dv):
    it = iter(refs)
    q_ref = next(it)
    if mode == "hgrn":
        zf_ref, zb_ref, v_ref, og_ref, lbf_ref, lbb_ref = (next(it) for _ in range(6))
    else:
        k_ref, v_ref, og_ref, lr_ref, w2f_ref, w2b_ref, gbf_ref, gbb_ref = (next(it) for _ in range(8))
    if has_s0:
        s0f_ref, s0b_ref = next(it), next(it)
    onorm_ref, lvf_ref, lvb_ref = next(it), next(it), next(it)
    o_ref = next(it)
    if emit_state:
        sout_ref = next(it)
    oacc_ref, stf_ref, stb_ref = next(it), next(it), next(it)
    n = t // CHUNK
    half = n // 2

    def chunk_rows(step, rev):
        ci = (n - 1 - step) if rev else step
        return pl.ds(pl.multiple_of(ci * CHUNK, CHUNK), CHUNK)

    def local(step, rev):
        rows = chunk_rows(step, rev)
        q = q_ref[0, rows, :]
        if mode == "hgrn":
            z = (zb_ref if rev else zf_ref)[0, rows, :]
            lb = (lbb_ref if rev else lbf_ref)[...]
            f = lb + (1.0 - lb) * jax.nn.sigmoid(z)
            k = 1.0 - f
            g = jnp.log2(f)
        else:
            k = k_ref[0, rows, :]
            q = q * (dk ** -0.5)
            x = jnp.dot(lr_ref[0, rows, :].astype(BF16), (w2b_ref if rev else w2f_ref)[...],
                        preferred_element_type=F32) + (gbb_ref if rev else gbf_ref)[...]
            g = jax.nn.log_sigmoid(x) * (LOG2E / C_GATE_NORM)
        return _chunk_local(q, k, g, (lvb_ref if rev else lvf_ref)[...], rev)

    if has_s0:
        stf_ref[...] = s0f_ref[0, 0, 0].T
        stb_ref[...] = s0b_ref[0, 0, 0].T
    else:
        stf_ref[...] = jnp.zeros_like(stf_ref)
        stb_ref[...] = jnp.zeros_like(stb_ref)

    unroll = 4 if half % 4 == 0 else (2 if half % 2 == 0 else 1)
    trips, half_trips = n // unroll, half // unroll
    lanes = [(u, rev) for u in range(unroll) for rev in (False, True)]

    def local_stage(trip):
        return tuple(local(trip * unroll + u, rev) for u, rev in lanes)

    def state_stage(trip, locals_, final):
        for (u, rev), loc in zip(lanes, locals_):
            rows = chunk_rows(trip * unroll + u, rev)
            st_ref = stb_ref if rev else stf_ref
            o, st_new = _chunk_state(loc, v_ref[0, rows, :], st_ref[...])
            st_ref[...] = st_new
            if not final:
                oacc_ref[rows, :] = o
            else:
                o = o + oacc_ref[rows, :]
                o = o * lax.rsqrt(jnp.mean(o * o, axis=-1, keepdims=True) + EPS) * onorm_ref[...]
                og = og_ref[0, rows, :]
                o_ref[0, rows, :] = (o * (og * jax.nn.sigmoid(og))).astype(o_ref.dtype)

    def body(trip, carry, final):
        nxt = local_stage(trip + 1)
        state_stage(trip, carry, final)
        return nxt

    carry = local_stage(0)
    carry = lax.fori_loop(0, half_trips, functools.partial(body, final=False), carry)
    carry = lax.fori_loop(half_trips, trips - 1, functools.partial(body, final=True), carry)
    state_stage(trips - 1, carry, True)
    if emit_state:
        sout_ref[0, 0, 0] = stf_ref[...].T
        sout_ref[0, 1, 0] = stb_ref[...].T


def _scan(mode, proj, col0, onorm, *, heads, dk, dv, extra, s0=None, emit_state=False):
    bsz, t, _ = proj.shape

    def col(name, width):
        base = col0[name] // width
        return pl.BlockSpec((1, t, width), lambda b, h: (b, 0, base + h))

    args, specs = [proj], [col("q", dk)]
    if mode == "hgrn":
        lbf, lbb = extra
        args += [proj, proj, proj, proj, lbf, lbb]
        specs += [col("zf", dk), col("zb", dk), col("v", dv), col("og", dv),
                  pl.BlockSpec((1, dk), lambda b, h: (0, h)), pl.BlockSpec((1, dk), lambda b, h: (0, h))]
    else:
        lr, w2f, w2b, gbf, gbb = extra
        args += [proj, proj, proj, lr, w2f, w2b, gbf, gbb]
        specs += [col("k", dk), col("v", dv), col("og", dv),
                  pl.BlockSpec((1, t, LANES), lambda b, h: (b, 0, 0)),
                  pl.BlockSpec((LANES, dk), lambda b, h: (0, h)), pl.BlockSpec((LANES, dk), lambda b, h: (0, h)),
                  pl.BlockSpec((1, dk), lambda b, h: (0, h)), pl.BlockSpec((1, dk), lambda b, h: (0, h))]
    st_block = (1, 1, 1, dk, dv)
    if s0 is not None:
        args += [s0, s0]
        specs += [pl.BlockSpec(st_block, lambda b, h: (b, 0, h, 0, 0)),
                  pl.BlockSpec(st_block, lambda b, h: (b, 1, h, 0, 0))]
    args += [onorm.reshape(1, dv), _level_table(False), _level_table(True)]
    specs += [pl.BlockSpec((1, dv), lambda b, h: (0, 0)),
              pl.BlockSpec((CHUNK, CHUNK), lambda b, h: (0, 0)),
              pl.BlockSpec((CHUNK, CHUNK), lambda b, h: (0, 0))]
    out_shape = [jax.ShapeDtypeStruct((bsz, t, heads * dv), BF16)]
    out_specs = [pl.BlockSpec((1, t, dv), lambda b, h: (b, 0, h))]
    if emit_state:
        out_shape += [jax.ShapeDtypeStruct((bsz, 2, heads, dk, dv), F32)]
        out_specs += [pl.BlockSpec((1, 2, 1, dk, dv), lambda b, h: (b, 0, h, 0, 0))]
    kern = functools.partial(_scan_kernel, mode=mode, has_s0=s0 is not None, emit_state=emit_state,
                             t=t, dk=dk, dv=dv)
    return pl.pallas_call(
        kern, out_shape=out_shape, grid=(bsz, heads), in_specs=specs, out_specs=out_specs,
        scratch_shapes=[pltpu.VMEM((t, dv), F32), pltpu.VMEM((dv, dk), F32), pltpu.VMEM((dv, dk), F32)],
        compiler_params=_params(("arbitrary", "arbitrary")), name="gated_scan_" + mode,
    )(*args)


DMA_GROUP = 8


def _moe_kernel(idx_ref, h_hbm, gate_ref, wg_ref, wu_ref, wd_ref, zero_hbm, out_hbm, *rest,
                tm, nf, tf, nd, tn, emit_bf16):
    if emit_bf16:
        wgb_ref, wub_ref, wdb_ref = rest[:3]
        rest = rest[3:]
    xrow_ref, orow_ref, xb_ref, hid_ref, sem_x, sem_o = rest
    del zero_hbm
    s = pl.program_id(2)
    tile = pl.program_id(0) * pl.num_programs(1) + pl.program_id(1)
    n_tiles = pl.num_programs(0) * pl.num_programs(1)
    base = tile * tm
    next_base = jnp.minimum(base + tm, (n_tiles - 1) * tm)
    prev_base = jnp.maximum(base - tm, 0)
    half = nf // 2
    rows_x, rows_o = tm // nf, tm // half

    def x_copy(b0, grp, u):
        tok = idx_ref[b0 + grp * rows_x + u]
        return pltpu.make_async_copy(h_hbm.at[pl.ds(tok, 1)], xrow_ref.at[grp, pl.ds(u, 1)], sem_x)

    def o_gather(grp, u):
        tok = idx_ref[base + grp * rows_o + u]
        return pltpu.make_async_copy(out_hbm.at[pl.ds(tok, 1)], orow_ref.at[grp, pl.ds(u, 1)], sem_o)

    def o_scatter(b0, grp, u):
        tok = idx_ref[b0 + grp * rows_o + u]
        return pltpu.make_async_copy(orow_ref.at[grp, pl.ds(u, 1)], out_hbm.at[pl.ds(tok, 1)], sem_o)

    def for_rows(fn, rows):
        per = rows // DMA_GROUP
        def body(i, c):
            for u in range(DMA_GROUP):
                fn(i // per, (i % per) * DMA_GROUP + u)
            return c
        lax.fori_loop(0, tm // DMA_GROUP, body, 0)

    @pl.when(s == 0)
    def _():
        @pl.when(tile == 0)
        def _():
            orow_ref[...] = jnp.zeros_like(orow_ref)
            for_rows(lambda g, u: x_copy(base, g, u).start(), rows_x)
        for_rows(lambda g, u: x_copy(base, g, u).wait(), rows_x)
        xb_ref[...] = xrow_ref[...].reshape(tm, xb_ref.shape[1]).astype(BF16)

    def up_step(first_half):
        for u in range(rows_x):
            x_copy(next_base, s, u).start()
        for u in range(rows_o):
            if first_half:
                o_scatter(prev_base, s, u).start()
            else:
                o_gather(s - half, u).start()
        x = xb_ref[...]
        wg, wu = wg_ref[0, 0], wu_ref[0, 0]
        if emit_bf16:
            wg, wu = wg.astype(BF16), wu.astype(BF16)
            wgb_ref[0] = wg
            wub_ref[0] = wu
        if 2 * tf <= MXU_DIM:
            gu = jnp.dot(x, jnp.concatenate([wg, wu], axis=1), preferred_element_type=F32)
            g, u = gu[:, :tf], gu[:, tf:]
        else:
            g = jnp.dot(x, wg, preferred_element_type=F32)
            u = jnp.dot(x, wu, preferred_element_type=F32)
        hid_ref[s] = (g * jax.nn.sigmoid(g) * u).astype(BF16)

    @pl.when(s < half)
    def _():
        up_step(True)

    @pl.when(s == half)
    def _():
        for_rows(lambda g, u: o_scatter(prev_base, g, u).wait(), rows_o)

    @pl.when((s >= half) & (s < nf))
    def _():
        up_step(False)

    for n in range(nd):
        @pl.when(s == nf + n)
        def _(n=n):
            if n == 0:
                for_rows(lambda g, u: o_gather(g, u).wait(), rows_o)
            if emit_bf16:
                wdb_ref[0] = wd_ref[0, 0].astype(BF16)
                wd = wdb_ref.at[0]
            else:
                wd = wd_ref.at[0, 0]
            per = max(1, MXU_DIM // tf)
            acc = None
            for f in range(0, nf, per):
                lhs = hid_ref[f] if per == 1 else jnp.concatenate([hid_ref[f + i] for i in range(per)], axis=1)
                part = jnp.dot(lhs, wd[f * tf:(f + per) * tf, :], preferred_element_type=F32)
                acc = part if acc is None else acc + part
            orow_ref[:, :, n * tn:(n + 1) * tn] += (acc * gate_ref[0]).reshape(half, rows_o, tn)
            if n == nd - 1:
                @pl.when(tile == n_tiles - 1)
                def _():
                    for_rows(lambda g, u: o_scatter(base, g, u).start(), rows_o)
                    for_rows(lambda g, u: o_scatter(base, g, u).wait(), rows_o)
                    for_rows(lambda g, u: x_copy(base, g, u).wait(), rows_x)


def _moe(h, idx, gate, wg, wu, wd, li, emit_bf16=False, tm=512):
    n, d = h.shape
    n_e, cap = idx.shape
    dexp = wg.shape[3]
    tf, tn = (128, 512) if emit_bf16 else (256, 1024)
    tm, tf, tn = min(tm, cap), min(tf, dexp), min(tn, d)
    nf, nd = dexp // tf, d // tn
    assert nf % 2 == 0 and tm % nf == 0 and tm % DMA_GROUP == 0
    assert not emit_bf16 or cap == tm
    out_shape = [jax.ShapeDtypeStruct((n, d), F32)]
    out_specs = [pl.BlockSpec(memory_space=pl.ANY)]
    if emit_bf16:
        out_shape += [jax.ShapeDtypeStruct((n_e, d, dexp), BF16)] * 2 + [jax.ShapeDtypeStruct((n_e, dexp, d), BF16)]
        out_specs += [pl.BlockSpec((1, d, tf), lambda e, m, s, idx: (e, 0, jnp.minimum(s, nf - 1)))] * 2
        out_specs += [pl.BlockSpec((1, dexp, tn), lambda e, m, s, idx: (e, 0, jnp.maximum(s - nf, 0)))]
    grid_spec = pltpu.PrefetchScalarGridSpec(
        num_scalar_prefetch=1,
        grid=(n_e, cap // tm, nf + nd),
        in_specs=[pl.BlockSpec(memory_space=pl.ANY),
                  pl.BlockSpec((1, tm, 1), lambda e, m, s, idx: (e, m, 0)),
                  pl.BlockSpec((1, 1, d, tf), lambda e, m, s, idx: (li, e, 0, jnp.minimum(s, nf - 1))),
                  pl.BlockSpec((1, 1, d, tf), lambda e, m, s, idx: (li, e, 0, jnp.minimum(s, nf - 1))),
                  pl.BlockSpec((1, 1, dexp, tn), lambda e, m, s, idx: (li, e, 0, jnp.maximum(s - nf, 0))),
                  pl.BlockSpec(memory_space=pl.ANY)],
        out_specs=out_specs,
        scratch_shapes=[pltpu.VMEM((nf, tm // nf, d), F32), pltpu.VMEM((nf // 2, 2 * tm // nf, d), F32),
                        pltpu.VMEM((tm, d), BF16),
                        pltpu.VMEM((nf, tm, tf), BF16), pltpu.SemaphoreType.DMA, pltpu.SemaphoreType.DMA],
    )
    return pl.pallas_call(
        functools.partial(_moe_kernel, tm=tm, nf=nf, tf=tf, nd=nd, tn=tn, emit_bf16=emit_bf16),
        out_shape=out_shape, grid_spec=grid_spec,
        input_output_aliases={6: 0},
        compiler_params=_params(("arbitrary", "arbitrary", "arbitrary")), name="moe_ffn",
    )(idx.reshape(-1), h, gate.reshape(n_e, cap, 1), wg, wu, wd, jnp.zeros((n, d), F32))


def _mods(ada, lo, hi):
    d = ada.shape[1] // 6
    return [ada[lo:hi, k * d:(k + 1) * d].reshape(hi - lo, 1, d) for k in range(6)]


def _ab_mixer(h, w_in, lam_p, subln, lb, onorm, lam_init, rope=None, ctx=None):
    bsz, t, d = h.shape
    proj = _mm([h.reshape(bsz * t, d)], w_in).reshape(bsz, t, -1)
    bw = B_HEADS * B_KEY
    col0 = {"q": 3 * A_WIDTH, "zf": 3 * A_WIDTH + bw, "zb": 3 * A_WIDTH + 2 * bw,
            "v": 3 * A_WIDTH + 3 * bw, "og": 3 * A_WIDTH + 4 * bw}
    extra = (lb[0:1], lb[1:2])
    if ctx is None:
        o_a, k_c, v_c = _attention(proj, lam_p, subln, lam_init)
        o_b, states = _scan("hgrn", proj, col0, onorm, heads=B_HEADS, dk=B_KEY, dv=B_VAL, extra=extra,
                            emit_state=True)
        return (o_a, o_b), (k_c, v_c, states)
    ck, cv, cs = ctx
    (o_a,) = _attention(proj, lam_p, subln, lam_init, ctx=(ck, cv), rope=rope)
    (o_b,) = _scan("hgrn", proj, col0, onorm, heads=B_HEADS, dk=B_KEY, dv=B_VAL, extra=extra, s0=cs)
    return (o_a, o_b), None


def _gla_mixer(h, w_main, w_lr, w2f, w2b, gbf, gbb, onorm, ctx_state=None):
    bsz, t, d = h.shape
    h2d = h.reshape(bsz * t, d)
    n_main = C_HEADS * (2 * C_KEY + 2 * C_VAL)
    proj = _mm([h2d], w_main, n_cols=n_main).reshape(bsz, t, n_main)
    lr = _mm([h2d], w_lr).reshape(bsz, t, LANES)
    col0 = {"q": 0, "k": C_HEADS * C_KEY, "v": 2 * C_HEADS * C_KEY, "og": 2 * C_HEADS * C_KEY + C_HEADS * C_VAL}
    extra = (lr, w2f, w2b, gbf, gbb)
    if ctx_state is None:
        o, states = _scan("gla", proj, col0, onorm, heads=C_HEADS, dk=C_KEY, dv=C_VAL, extra=extra,
                          emit_state=True)
        return (o,), states
    (o,) = _scan("gla", proj, col0, onorm, heads=C_HEADS, dk=C_KEY, dv=C_VAL, extra=extra, s0=ctx_state)
    return (o,), None


def kernel(x_prompt, x_sample, cache_k_a, cache_v_a, state_hgrn, state_gla, c, c_ctx, ada_w, ada_b, norm_mix, norm_ffn, ab_w_in, ab_lambda, ab_subln, ab_lb_logits, ab_onorm, ab_w_out, c_w_in, c_gate_w2, c_gate_b, c_onorm, c_w_out, router_w, exp_w_gate, exp_w_up, exp_w_down, final_norm):
    d = x_prompt.shape[-1]
    n_s = c.shape[0]
    lb_table = jnp.cumsum(jax.nn.softmax(ab_lb_logits.astype(F32), axis=0), axis=0)
    rope = _rope_tables(x_sample.shape[1])
    cond_rows = 8 * ((1 + n_s + 7) // 8)
    cond = jnp.concatenate([c_ctx[None, :], c, jnp.zeros((cond_rows - 1 - n_s, d), F32)], axis=0)

    ys = [x_prompt, x_sample]
    pending = [None, None]
    new_k = new_v = new_hgrn = new_gla = None
    for li in range(DEPTH):
        ada = _adaln(cond, ada_w, ada_b, li)
        mods = [_mods(ada, 0, 1), _mods(ada, 1, 1 + n_s)]
        j = li // 2
        if li % 2 == 0:
            lam_init = 0.8 - 0.6 * math.exp(-0.3 * li)
            w_in = ab_w_in[j].astype(BF16)
            w_out = ab_w_out[j].astype(BF16)
        else:
            n_main = C_HEADS * (2 * C_KEY + 2 * C_VAL)
            w_in = c_w_in[j].astype(BF16)
            w_lr = jnp.pad(c_w_in[j][:, n_main:], ((0, 0), (0, LANES - 2 * C_RANK))).astype(BF16)
            w2 = c_gate_w2[j]
            w2f = jnp.pad(w2[0], ((0, LANES - C_RANK), (0, 0))).astype(BF16)
            w2b = jnp.pad(w2[1], ((C_RANK, LANES - 2 * C_RANK), (0, 0))).astype(BF16)
            gbf, gbb = c_gate_b[j][0:1], c_gate_b[j][1:2]
            w_out = c_w_out[j].astype(BF16)
        for gi in range(2):
            sh1, sc1, g1, sh2, sc2, g2 = mods[gi]
            y = ys[gi]
            bsz, t, _ = y.shape
            if pending[gi] is None:
                (h,) = _norm(y, norm_mix[li], mod=(sc1, sh1), out_dtype=BF16)
            else:
                y, h = _norm(y, norm_mix[li], resid=pending[gi], mod=(sc1, sh1), emit_y=True, out_dtype=BF16)
            is_ctx = gi == 0
            if li % 2 == 0:
                ctx = None if is_ctx else (cache_k_a[:, j], cache_v_a[:, j], state_hgrn[:, j])
                outs, cache = _ab_mixer(h, w_in, ab_lambda[j], ab_subln[j], lb_table[li], ab_onorm[j], lam_init,
                                        rope=rope, ctx=ctx)
                if is_ctx:
                    new_k, new_v, new_hgrn = cache
            else:
                outs, cache = _gla_mixer(h, w_in, w_lr, w2f, w2b, gbf, gbb, c_onorm[j],
                                         ctx_state=None if is_ctx else state_gla[:, j])
                if is_ctx:
                    new_gla = cache
            xs = [o.reshape(bsz * t, -1) for o in outs]
            y1 = _mm(xs, w_out, resid=(y.reshape(bsz * t, d), g1), rows_per_cond=t).reshape(bsz, t, d)
            h2, aff = _norm(y1, norm_ffn[li], mod=(sc2, sh2), router_w=router_w[li], out_dtype=F32)
            n_tok = bsz * t
            n_e = router_w.shape[-1]
            cap = EC_FACTOR * n_tok // n_e
            gate, idx = lax.top_k(aff.reshape(n_tok, ROUTER_PAD)[:, :n_e].T, cap)
            if gi == 0:
                moe, wg, wu, wd = _moe(h2.reshape(n_tok, d), idx, gate, exp_w_gate, exp_w_up, exp_w_down, li,
                                       emit_bf16=True)
            else:
                (moe,) = _moe(h2.reshape(n_tok, d), idx, gate, wg[None], wu[None], wd[None], 0)
            moe = moe.reshape(bsz, t, d)
            ys[gi] = y1
            pending[gi] = (moe, g2)
    outs = []
    for gi in range(2):
        (yo,) = _norm(ys[gi], final_norm, resid=pending[gi], out_dtype=F32)
        outs.append(yo)
    bp, tp = x_prompt.shape[:2]
    return (outs[0], outs[1],
            new_k.reshape(bp, tp, A_HEADS, 2 * A_QK)[:, None],
            new_v.reshape(bp, tp, A_HEADS, A_V)[:, None],
            new_hgrn[:, None], new_gla[:, None])
```

```python
import functools
import math

import numpy as np
import jax
import jax.numpy as jnp
from jax import lax
from jax.experimental import pallas as pl
from jax.experimental.pallas import tpu as pltpu

D_MODEL = 4096
DEPTH = 2
GRID_W = 64
A_HEADS = 8
A_V = 256
A_QK = 128
A_WIDTH = A_HEADS * A_V
B_KEY = 128
B_VAL = 128
B_WIDTH = D_MODEL // 2
B_HEADS = B_WIDTH // B_KEY
C_HEADS = 8
C_KEY = D_MODEL // 2 // C_HEADS
C_VAL = D_MODEL // C_HEADS
C_RANK = 16
C_GATE_NORM = 16.0
N_EXPERTS = 16
EC_FACTOR = 2
D_EXPERT = D_MODEL // 2
CHUNK = 64
ROPE_BASE = 10000.0
EPS = 1e-6
LOG2E = math.log2(math.e)
F32 = jnp.float32
BF16 = jnp.bfloat16

LANES = 128
MXU_DIM = 256
VMEM_LIMIT = 56 * 1024 * 1024
ROUTER_PAD = LANES


def _params(sem, vmem=VMEM_LIMIT):
    return pltpu.CompilerParams(dimension_semantics=sem, vmem_limit_bytes=vmem)


def _adaln_kernel(c_ref, w_ref, b_ref, o_ref):
    c = c_ref[...]
    a = (c * jax.nn.sigmoid(c)).astype(BF16)
    o_ref[...] = jnp.dot(a, w_ref[...].astype(BF16), preferred_element_type=F32) + b_ref[...]


def _adaln(cond, w, b, li):
    rows, d = cond.shape
    n_layers, _, n = w.shape
    tn = 512
    return pl.pallas_call(
        _adaln_kernel,
        out_shape=jax.ShapeDtypeStruct((rows, n), F32),
        grid=(n // tn,),
        in_specs=[pl.BlockSpec((rows, d), lambda j: (0, 0)),
                  pl.BlockSpec((None, d, tn), lambda j: (li, 0, j)),
                  pl.BlockSpec((None, 1, tn), lambda j: (li, 0, j))],
        out_specs=pl.BlockSpec((rows, tn), lambda j: (0, j)),
        compiler_params=_params(("arbitrary",)),
        name="adaln",
    )(cond, w, b.reshape(n_layers, 1, n))


def _norm_kernel(*refs, has_resid, has_mod, has_router, emit_y, n_experts):
    it = iter(refs)
    y_ref = next(it)
    if has_resid:
        m_ref, gate_ref = next(it), next(it)
    g_ref = next(it)
    if has_mod:
        sc_ref, sh_ref = next(it), next(it)
    if has_router:
        rwh_ref, rwl_ref = next(it), next(it)
    if emit_y:
        yo_ref = next(it)
    h_ref = next(it)
    if has_router:
        aff_ref = next(it)

    x = y_ref[0]
    if has_resid:
        x = x + gate_ref[0] * m_ref[0]
    if emit_y:
        yo_ref[0] = x
    ms = jnp.mean(x * x, axis=-1, keepdims=True)
    h = (x * lax.rsqrt(ms + EPS)) * g_ref[...]
    if has_mod:
        h = h * (1.0 + sc_ref[0]) + sh_ref[0]
    h_ref[0] = h.astype(h_ref.dtype)
    if has_router:
        h_hi = h.astype(BF16)
        h_lo = (h - h_hi.astype(F32)).astype(BF16)
        logits = (jnp.dot(h_hi, rwh_ref[...], preferred_element_type=F32)
                  + (jnp.dot(h_hi, rwl_ref[...], preferred_element_type=F32)
                     + jnp.dot(h_lo, rwh_ref[...], preferred_element_type=F32)))
        col = lax.broadcasted_iota(jnp.int32, logits.shape, 1)
        logits = jnp.where(col < n_experts, logits, -jnp.inf)
        e = jnp.exp(logits - jnp.max(logits, axis=-1, keepdims=True))
        aff_ref[0] = e / jnp.sum(e, axis=-1, keepdims=True)


def _norm(y, gnorm, *, resid=None, mod=None, router_w=None, emit_y=False, out_dtype=F32, tt=256):
    bsz, t, d = y.shape
    tt = min(tt, t)
    tok = pl.BlockSpec((1, tt, d), lambda b, i: (b, i, 0))

    def cond_spec(a):
        if a.shape[0] == 1:
            return pl.BlockSpec((1, 1, d), lambda b, i: (0, 0, 0))
        return pl.BlockSpec((1, 1, d), lambda b, i: (b, 0, 0))

    args, specs = [y], [tok]
    if resid is not None:
        args += [resid[0], resid[1]]
        specs += [tok, cond_spec(resid[1])]
    args.append(gnorm.reshape(1, d))
    specs.append(pl.BlockSpec((1, d), lambda b, i: (0, 0)))
    if mod is not None:
        args += [mod[0], mod[1]]
        specs += [cond_spec(mod[0]), cond_spec(mod[1])]
    if router_w is not None:
        rw = jnp.pad(router_w, ((0, 0), (0, ROUTER_PAD - router_w.shape[1])))
        rw_hi = rw.astype(BF16)
        args += [rw_hi, (rw - rw_hi.astype(F32)).astype(BF16)]
        specs += [pl.BlockSpec((d, ROUTER_PAD), lambda b, i: (0, 0))] * 2
    out_shape, out_specs = [], []
    if emit_y:
        out_shape.append(jax.ShapeDtypeStruct((bsz, t, d), F32))
        out_specs.append(tok)
    out_shape.append(jax.ShapeDtypeStruct((bsz, t, d), out_dtype))
    out_specs.append(tok)
    if router_w is not None:
        out_shape.append(jax.ShapeDtypeStruct((bsz, t, ROUTER_PAD), F32))
        out_specs.append(pl.BlockSpec((1, tt, ROUTER_PAD), lambda b, i: (b, i, 0)))
    kern = functools.partial(_norm_kernel, has_resid=resid is not None, has_mod=mod is not None,
                             has_router=router_w is not None, emit_y=emit_y,
                             n_experts=0 if router_w is None else router_w.shape[1])
    return pl.pallas_call(
        kern, out_shape=out_shape, grid=(bsz, t // tt), in_specs=specs, out_specs=out_specs,
        compiler_params=_params(("arbitrary", "arbitrary")), name="norm",
    )(*args)


def _mm_kernel(*refs, n_in, has_resid):
    xs = refs[:n_in]
    ws = refs[n_in:2 * n_in]
    rest = refs[2 * n_in:]
    acc = jnp.dot(xs[0][...], ws[0][...], preferred_element_type=F32)
    for x_ref, w_ref in zip(xs[1:], ws[1:]):
        acc = acc + jnp.dot(x_ref[...], w_ref[...], preferred_element_type=F32)
    if has_resid:
        y_ref, gate_ref, o_ref = rest
        o_ref[...] = y_ref[...] + gate_ref[0] * acc
    else:
        (o_ref,) = rest
        o_ref[...] = acc.astype(o_ref.dtype)


def _mm(xs, w, *, n_cols=None, resid=None, rows_per_cond=None, out_dtype=F32, tm=1024, tn=1024):
    m = xs[0].shape[0]
    n = w.shape[1] if n_cols is None else n_cols
    tm, tn = min(tm, m), min(tn, n)
    if resid is not None and resid[1].shape[0] > 1:
        tm = min(tm, rows_per_cond)
    specs, args = [], []
    for x in xs:
        args.append(x)
        specs.append(pl.BlockSpec((tm, x.shape[1]), lambda i, j: (i, 0)))
    kb = 0
    for x in xs:
        k = x.shape[1]
        args.append(w)
        specs.append(pl.BlockSpec((k, tn), functools.partial(lambda i, j, r: (r, j), r=kb // k)))
        kb += k
    if resid is not None:
        y, gate = resid
        args += [y, gate]
        specs.append(pl.BlockSpec((tm, tn), lambda i, j: (i, j)))
        if gate.shape[0] == 1:
            specs.append(pl.BlockSpec((1, 1, tn), lambda i, j: (0, 0, j)))
        else:
            per = rows_per_cond // tm
            specs.append(pl.BlockSpec((1, 1, tn), lambda i, j: (i // per, 0, j)))
    kern = functools.partial(_mm_kernel, n_in=len(xs), has_resid=resid is not None)
    return pl.pallas_call(
        kern, out_shape=jax.ShapeDtypeStruct((m, n), out_dtype), grid=(m // tm, n // tn),
        in_specs=specs, out_specs=pl.BlockSpec((tm, tn), lambda i, j: (i, j)),
        compiler_params=_params(("arbitrary", "arbitrary")), name="matmul",
    )(*args)


def _rope_tables(n_tokens):
    rows = n_tokens // GRID_W
    row = jnp.repeat(jnp.arange(rows, dtype=F32), GRID_W)
    col = jnp.tile(jnp.arange(GRID_W, dtype=F32), rows)
    axis_dim = A_QK // 2
    inv = ROPE_BASE ** (-jnp.arange(0, axis_dim, 2, dtype=F32) / axis_dim)
    ar, ac = row[:, None] * inv, col[:, None] * inv
    cos = jnp.concatenate([jnp.cos(ar), jnp.cos(ar), jnp.cos(ac), jnp.cos(ac)], axis=-1)
    sin = jnp.concatenate([-jnp.sin(ar), jnp.sin(ar), -jnp.sin(ac), jnp.sin(ac)], axis=-1)
    return cos, sin


def _rope(x, cos, sin):
    q = A_QK // 4
    lane = lax.broadcasted_iota(jnp.int32, x.shape, 1)
    first = (lane % (2 * q)) < q
    partner = jnp.where(first, pltpu.roll(x, A_QK - q, axis=1), pltpu.roll(x, q, axis=1))
    return x * cos + partner * sin


def _attn_kernel(*refs, use_ctx, emit_kv, lam_init, t_own):
    it = iter(refs)
    q_ref, k_ref, v_ref = next(it), next(it), next(it)
    if use_ctx:
        ck_ref, cv_ref, cq_ref, sq_ref, ckk_ref, skk_ref = (next(it) for _ in range(6))
    lam_ref, sub_ref = next(it), next(it)
    o_ref = next(it)
    if emit_kv:
        nk_ref, nv_ref = next(it), next(it)
    kb_ref, vb_ref = next(it), next(it)

    @pl.when(pl.program_id(2) == 0)
    def _():
        k = k_ref[0]
        v = v_ref[0]
        if emit_kv:
            nk_ref[0] = k
            nv_ref[0] = v
        for mp in range(2):
            km = k[:, mp * A_QK:(mp + 1) * A_QK]
            if use_ctx:
                km = _rope(km, ckk_ref[...], skk_ref[...])
            kb_ref[0:t_own, mp * A_QK:(mp + 1) * A_QK] = km.astype(BF16)
        vb_ref[0:t_own, :] = v.astype(BF16)
        if use_ctx:
            kb_ref[t_own:, :] = ck_ref[0].astype(BF16)
            vb_ref[t_own:, :] = cv_ref[0].astype(BF16)

    lp = lam_ref[...]
    lam = (jnp.exp(jnp.sum(lp[0:1] * lp[1:2], axis=-1, keepdims=True))
           - jnp.exp(jnp.sum(lp[2:3] * lp[3:4], axis=-1, keepdims=True)) + lam_init)
    q = q_ref[0]
    scale = (A_QK ** -0.5) * LOG2E
    es, ls = [], []
    for mp in range(2):
        qm = q[:, mp * A_QK:(mp + 1) * A_QK]
        if use_ctx:
            qm = _rope(qm, cq_ref[...], sq_ref[...])
        qm = (qm * scale).astype(BF16)
        s = lax.dot_general(qm, kb_ref[:, mp * A_QK:(mp + 1) * A_QK], (((1,), (1,)), ((), ())),
                            preferred_element_type=F32)
        e = jnp.exp2(s - jnp.max(s, axis=-1, keepdims=True))
        es.append(e)
        ls.append(jnp.sum(e, axis=-1, keepdims=True))
    w = es[0] * (1.0 / ls[0]) - es[1] * (lam / ls[1])
    o = jnp.dot(w.astype(BF16), vb_ref[...], preferred_element_type=F32)
    o = o * lax.rsqrt(jnp.mean(o * o, axis=-1, keepdims=True) + EPS) * sub_ref[...]
    o_ref[0] = (o * (1.0 - lam_init)).astype(o_ref.dtype)


def _attention(proj, lam_p, subln, lam_init, ctx=None, rope=None, tq=256):
    bsz, t, _ = proj.shape
    tq = min(tq, t)
    hw = 2 * A_QK
    use_ctx = ctx is not None
    args = [proj, proj, proj]
    specs = [pl.BlockSpec((1, tq, hw), lambda b, h, i: (b, i, h)),
             pl.BlockSpec((1, t, hw), lambda b, h, i: (b, 0, A_HEADS + h)),
             pl.BlockSpec((1, t, A_V), lambda b, h, i: (b, 0, 2 * A_HEADS + h))]
    t_all = t
    if use_ctx:
        ck, cv = ctx
        p = ck.shape[1]
        t_all = t + p
        cos, sin = rope
        args += [ck.reshape(bsz, p, A_HEADS * hw), cv.reshape(bsz, p, A_WIDTH), cos, sin, cos, sin]
        specs += [pl.BlockSpec((1, p, hw), lambda b, h, i: (b, 0, h)),
                  pl.BlockSpec((1, p, A_V), lambda b, h, i: (b, 0, h)),
                  pl.BlockSpec((tq, A_QK), lambda b, h, i: (i, 0)),
                  pl.BlockSpec((tq, A_QK), lambda b, h, i: (i, 0)),
                  pl.BlockSpec((t, A_QK), lambda b, h, i: (0, 0)),
                  pl.BlockSpec((t, A_QK), lambda b, h, i: (0, 0))]
    args += [lam_p, subln.reshape(1, A_V)]
    specs += [pl.BlockSpec((4, A_QK), lambda b, h, i: (0, 0)),
              pl.BlockSpec((1, A_V), lambda b, h, i: (0, 0))]
    out_shape = [jax.ShapeDtypeStruct((bsz, t, A_WIDTH), BF16)]
    out_specs = [pl.BlockSpec((1, tq, A_V), lambda b, h, i: (b, i, h))]
    if not use_ctx:
        out_shape += [jax.ShapeDtypeStruct((bsz, t, A_WIDTH), F32)] * 2
        out_specs += [pl.BlockSpec((1, t, hw), lambda b, h, i: (b, 0, h)),
                      pl.BlockSpec((1, t, A_V), lambda b, h, i: (b, 0, h))]
    kern = functools.partial(_attn_kernel, use_ctx=use_ctx, emit_kv=not use_ctx, lam_init=lam_init, t_own=t)
    return pl.pallas_call(
        kern, out_shape=out_shape, grid=(bsz, A_HEADS, t // tq), in_specs=specs, out_specs=out_specs,
        scratch_shapes=[pltpu.VMEM((t_all, hw), BF16), pltpu.VMEM((t_all, A_V), BF16)],
        compiler_params=_params(("arbitrary", "arbitrary", "arbitrary")), name="diff_attention",
    )(*args)


def _level_table(rev):
    i = np.arange(CHUNK)[:, None]
    j = np.arange(CHUNK)[None, :]
    x = i ^ j
    lvl = np.where(x == 0, 0, 2 ** np.floor(np.log2(np.maximum(x, 1))).astype(np.int64))
    keep = (i <= j) if rev else (i >= j)
    return jnp.asarray(np.where(keep, lvl, -1), dtype=jnp.int32)


def _chunk_local(q, k, g, lvl, rev):
    c, dk = g.shape
    row = lax.broadcasted_iota(jnp.int32, (c, dk), 0)
    pos = (c - 1 - row) if rev else row

    def prev(x, d):
        return pltpu.roll(x, (c - d) if rev else d, axis=0)

    sub = 8
    ng = c // sub
    p8 = pos % sub

    pre = g
    for d in (1, 2, 4):
        pre = pre + jnp.where(p8 >= d, prev(pre, d), 0.0)
    groups = [pre[sub * i:sub * (i + 1)] for i in range(ng)]
    tot_row = 0 if rev else sub - 1
    off = None
    b_groups = [None] * ng
    for i in (reversed(range(ng)) if rev else range(ng)):
        b_groups[i] = groups[i] if off is None else groups[i] + off
        tot = groups[i][tot_row:tot_row + 1, :]
        off = tot if off is None else off + tot
    b = jnp.concatenate(b_groups, axis=0)

    def group_row(p):
        a = (sub - 1 - p) if rev else p
        return jnp.concatenate([jnp.broadcast_to(gr[a:a + 1, :], (sub, dk)) for gr in groups], axis=0)

    ref2 = jnp.where(p8 < 4, group_row(1), group_row(5))
    ref4 = group_row(3)
    args = {
        1: jnp.where(p8 % 2 == 1, g, 0.0),
        2: jnp.where(p8 % 4 >= 2, pre - ref2, ref2 - pre),
        4: jnp.where(p8 >= 4, pre - ref4, ref4 - pre),
    }
    s = 8
    while s < c:
        parts = []
        for blk in range(c // (2 * s)):
            m = blk * 2 * s + (s if rev else s - 1)
            parts.append(jnp.broadcast_to(b[m:m + 1, :], (2 * s, dk)))
        bref = jnp.concatenate(parts, axis=0) if len(parts) > 1 else parts[0]
        args[s] = jnp.where(pos % (2 * s) >= s, b - bref, bref - b)
        s *= 2

    nt = (((1,), (1,)), ((), ()))
    a = jnp.where(lvl == 0, lax.dot_general(q.astype(BF16), k.astype(BF16), nt, preferred_element_type=F32), 0.0)
    for s, arg in args.items():
        e = jnp.exp2(arg)
        a_s = lax.dot_general((q * e).astype(BF16), (k * e).astype(BF16), nt, preferred_element_type=F32)
        a = jnp.where(lvl == s, a_s, a)

    last = 0 if rev else c - 1
    b_last = b[last:last + 1, :]
    q_dec = (q * jnp.exp2(b)).astype(BF16)
    k_dec = (k * jnp.exp2(b_last - b)).astype(BF16)
    return a.astype(BF16), q_dec, k_dec, jnp.exp2(b_last)


def _chunk_state(local, v, st):
    a, q_dec, k_dec, decay = local
    vb = v.astype(BF16)
    o = lax.dot_general(q_dec, st.astype(BF16), (((1,), (1,)), ((), ())), preferred_element_type=F32)
    o = o + jnp.dot(a, vb, preferred_element_type=F32)
    st_new = decay * st + lax.dot_general(vb, k_dec, (((0,), (0,)), ((), ())), preferred_element_type=F32)
    return o, st_new


def _scan_kernel(*refs, mode, has_s0, emit_state, t, dk, dv):
    it = iter(refs)
    q_ref = next(it)
    if mode == "hgrn":
        zf_ref, zb_ref, v_ref, og_ref, lbf_ref, lbb_ref = (next(it) for _ in range(6))
    else:
        k_ref, v_ref, og_ref, lr_ref, w2f_ref, w2b_ref, gbf_ref, gbb_ref = (next(it) for _ in range(8))
    if has_s0:
        s0f_ref, s0b_ref = next(it), next(it)
    onorm_ref, lvf_ref, lvb_ref = next(it), next(it), next(it)
    o_ref = next(it)
    if emit_state:
        sout_ref = next(it)
    oacc_ref, stf_ref, stb_ref = next(it), next(it), next(it)
    n = t // CHUNK
    half = n // 2

    def chunk_rows(step, rev):
        ci = (n - 1 - step) if rev else step
        return pl.ds(pl.multiple_of(ci * CHUNK, CHUNK), CHUNK)

    def local(step, rev):
        rows = chunk_rows(step, rev)
        q = q_ref[0, rows, :]
        if mode == "hgrn":
            z = (zb_ref if rev else zf_ref)[0, rows, :]
            lb = (lbb_ref if rev else lbf_ref)[...]
            f = lb + (1.0 - lb) * jax.nn.sigmoid(z)
            k = 1.0 - f
            g = jnp.log2(f)
        else:
            k = k_ref[0, rows, :]
            q = q * (dk ** -0.5)
            x = jnp.dot(lr_ref[0, rows, :].astype(BF16), (w2b_ref if rev else w2f_ref)[...],
                        preferred_element_type=F32) + (gbb_ref if rev else gbf_ref)[...]
            g = jax.nn.log_sigmoid(x) * (LOG2E / C_GATE_NORM)
        return _chunk_local(q, k, g, (lvb_ref if rev else lvf_ref)[...], rev)

    if has_s0:
        stf_ref[...] = s0f_ref[0, 0, 0].T
        stb_ref[...] = s0b_ref[0, 0, 0].T
    else:
        stf_ref[...] = jnp.zeros_like(stf_ref)
        stb_ref[...] = jnp.zeros_like(stb_ref)

    unroll = 4 if half % 4 == 0 else (2 if half % 2 == 0 else 1)
    trips, half_trips = n // unroll, half // unroll
    lanes = [(u, rev) for u in range(unroll) for rev in (False, True)]

    def local_stage(trip):
        return tuple(local(trip * unroll + u, rev) for u, rev in lanes)

    def state_stage(trip, locals_, final):
        for (u, rev), loc in zip(lanes, locals_):
            rows = chunk_rows(trip * unroll + u, rev)
            st_ref = stb_ref if rev else stf_ref
            o, st_new = _chunk_state(loc, v_ref[0, rows, :], st_ref[...])
            st_ref[...] = st_new
            if not final:
                oacc_ref[rows, :] = o
            else:
                o = o + oacc_ref[rows, :]
                o = o * lax.rsqrt(jnp.mean(o * o, axis=-1, keepdims=True) + EPS) * onorm_ref[...]
                og = og_ref[0, rows, :]
                o_ref[0, rows, :] = (o * (og * jax.nn.sigmoid(og))).astype(o_ref.dtype)

    def body(trip, carry, final):
        nxt = local_stage(trip + 1)
        state_stage(trip, carry, final)
        return nxt

    carry = local_stage(0)
    carry = lax.fori_loop(0, half_trips, functools.partial(body, final=False), carry)
    carry = lax.fori_loop(half_trips, trips - 1, functools.partial(body, final=True), carry)
    state_stage(trips - 1, carry, True)
    if emit_state:
        sout_ref[0, 0, 0] = stf_ref[...].T
        sout_ref[0, 1, 0] = stb_ref[...].T


def _scan(mode, proj, col0, onorm, *, heads, dk, dv, extra, s0=None, emit_state=False):
    bsz, t, _ = proj.shape

    def col(name, width):
        base = col0[name] // width
        return pl.BlockSpec((1, t, width), lambda b, h: (b, 0, base + h))

    args, specs = [proj], [col("q", dk)]
    if mode == "hgrn":
        lbf, lbb = extra
        args += [proj, proj, proj, proj, lbf, lbb]
        specs += [col("zf", dk), col("zb", dk), col("v", dv), col("og", dv),
                  pl.BlockSpec((1, dk), lambda b, h: (0, h)), pl.BlockSpec((1, dk), lambda b, h: (0, h))]
    else:
        lr, w2f, w2b, gbf, gbb = extra
        args += [proj, proj, proj, lr, w2f, w2b, gbf, gbb]
        specs += [col("k", dk), col("v", dv), col("og", dv),
                  pl.BlockSpec((1, t, LANES), lambda b, h: (b, 0, 0)),
                  pl.BlockSpec((LANES, dk), lambda b, h: (0, h)), pl.BlockSpec((LANES, dk), lambda b, h: (0, h)),
                  pl.BlockSpec((1, dk), lambda b, h: (0, h)), pl.BlockSpec((1, dk), lambda b, h: (0, h))]
    st_block = (1, 1, 1, dk, dv)
    if s0 is not None:
        args += [s0, s0]
        specs += [pl.BlockSpec(st_block, lambda b, h: (b, 0, h, 0, 0)),
                  pl.BlockSpec(st_block, lambda b, h: (b, 1, h, 0, 0))]
    args += [onorm.reshape(1, dv), _level_table(False), _level_table(True)]
    specs += [pl.BlockSpec((1, dv), lambda b, h: (0, 0)),
              pl.BlockSpec((CHUNK, CHUNK), lambda b, h: (0, 0)),
              pl.BlockSpec((CHUNK, CHUNK), lambda b, h: (0, 0))]
    out_shape = [jax.ShapeDtypeStruct((bsz, t, heads * dv), BF16)]
    out_specs = [pl.BlockSpec((1, t, dv), lambda b, h: (b, 0, h))]
    if emit_state:
        out_shape += [jax.ShapeDtypeStruct((bsz, 2, heads, dk, dv), F32)]
        out_specs += [pl.BlockSpec((1, 2, 1, dk, dv), lambda b, h: (b, 0, h, 0, 0))]
    kern = functools.partial(_scan_kernel, mode=mode, has_s0=s0 is not None, emit_state=emit_state,
                             t=t, dk=dk, dv=dv)
    return pl.pallas_call(
        kern, out_shape=out_shape, grid=(bsz, heads), in_specs=specs, out_specs=out_specs,
        scratch_shapes=[pltpu.VMEM((t, dv), F32), pltpu.VMEM((dv, dk), F32), pltpu.VMEM((dv, dk), F32)],
        compiler_params=_params(("arbitrary", "arbitrary")), name="gated_scan_" + mode,
    )(*args)


DMA_GROUP = 8


def _moe_kernel(idx_ref, h_hbm, gate_ref, wg_ref, wu_ref, wd_ref, zero_hbm, out_hbm, *rest,
                tm, nf, tf, nd, tn, emit_bf16):
    if emit_bf16:
        wgb_ref, wub_ref, wdb_ref = rest[:3]
        rest = rest[3:]
    xrow_ref, orow_ref, xb_ref, hid_ref, sem_x, sem_o = rest
    del zero_hbm
    s = pl.program_id(2)
    tile = pl.program_id(0) * pl.num_programs(1) + pl.program_id(1)
    n_tiles = pl.num_programs(0) * pl.num_programs(1)
    base = tile * tm
    next_base = jnp.minimum(base + tm, (n_tiles - 1) * tm)
    prev_base = jnp.maximum(base - tm, 0)
    half = nf // 2
    rows_x, rows_o = tm // nf, tm // half

    def x_copy(b0, grp, u):
        tok = idx_ref[b0 + grp * rows_x + u]
        return pltpu.make_async_copy(h_hbm.at[pl.ds(tok, 1)], xrow_ref.at[grp, pl.ds(u, 1)], sem_x)

    def o_gather(grp, u):
        tok = idx_ref[base + grp * rows_o + u]
        return pltpu.make_async_copy(out_hbm.at[pl.ds(tok, 1)], orow_ref.at[grp, pl.ds(u, 1)], sem_o)

    def o_scatter(b0, grp, u):
        tok = idx_ref[b0 + grp * rows_o + u]
        return pltpu.make_async_copy(orow_ref.at[grp, pl.ds(u, 1)], out_hbm.at[pl.ds(tok, 1)], sem_o)

    def for_rows(fn, rows):
        per = rows // DMA_GROUP
        def body(i, c):
            for u in range(DMA_GROUP):
                fn(i // per, (i % per) * DMA_GROUP + u)
            return c
        lax.fori_loop(0, tm // DMA_GROUP, body, 0)

    @pl.when(s == 0)
    def _():
        @pl.when(tile == 0)
        def _():
            orow_ref[...] = jnp.zeros_like(orow_ref)
            for_rows(lambda g, u: x_copy(base, g, u).start(), rows_x)
        for_rows(lambda g, u: x_copy(base, g, u).wait(), rows_x)
        xb_ref[...] = xrow_ref[...].reshape(tm, xb_ref.shape[1]).astype(BF16)

    def up_step(first_half):
        for u in range(rows_x):
            x_copy(next_base, s, u).start()
        for u in range(rows_o):
            if first_half:
                o_scatter(prev_base, s, u).start()
            else:
                o_gather(s - half, u).start()
        x = xb_ref[...]
        wg, wu = wg_ref[0, 0], wu_ref[0, 0]
        if emit_bf16:
            wg, wu = wg.astype(BF16), wu.astype(BF16)
            wgb_ref[0] = wg
            wub_ref[0] = wu
        if 2 * tf <= MXU_DIM:
            gu = jnp.dot(x, jnp.concatenate([wg, wu], axis=1), preferred_element_type=F32)
            g, u = gu[:, :tf], gu[:, tf:]
        else:
            g = jnp.dot(x, wg, preferred_element_type=F32)
            u = jnp.dot(x, wu, preferred_element_type=F32)
        hid_ref[s] = (g * jax.nn.sigmoid(g) * u).astype(BF16)

    @pl.when(s < half)
    def _():
        up_step(True)

    @pl.when(s == half)
    def _():
        for_rows(lambda g, u: o_scatter(prev_base, g, u).wait(), rows_o)

    @pl.when((s >= half) & (s < nf))
    def _():
        up_step(False)

    for n in range(nd):
        @pl.when(s == nf + n)
        def _(n=n):
            if n == 0:
                for_rows(lambda g, u: o_gather(g, u).wait(), rows_o)
            if emit_bf16:
                wdb_ref[0] = wd_ref[0, 0].astype(BF16)
                wd = wdb_ref.at[0]
            else:
                wd = wd_ref.at[0, 0]
            per = max(1, MXU_DIM // tf)
            acc = None
            for f in range(0, nf, per):
                lhs = hid_ref[f] if per == 1 else jnp.concatenate([hid_ref[f + i] for i in range(per)], axis=1)
                part = jnp.dot(lhs, wd[f * tf:(f + per) * tf, :], preferred_element_type=F32)
                acc = part if acc is None else acc + part
            orow_ref[:, :, n * tn:(n + 1) * tn] += (acc * gate_ref[0]).reshape(half, rows_o, tn)
            if n == nd - 1:
                @pl.when(tile == n_tiles - 1)
                def _():
                    for_rows(lambda g, u: o_scatter(base, g, u).start(), rows_o)
                    for_rows(lambda g, u: o_scatter(base, g, u).wait(), rows_o)
                    for_rows(lambda g, u: x_copy(base, g, u).wait(), rows_x)


def _moe(h, idx, gate, wg, wu, wd, li, emit_bf16=False, tm=512):
    n, d = h.shape
    n_e, cap = idx.shape
    dexp = wg.shape[3]
    tf, tn = (128, 512) if emit_bf16 else (256, 1024)
    tm, tf, tn = min(tm, cap), min(tf, dexp), min(tn, d)
    nf, nd = dexp // tf, d // tn
    assert nf % 2 == 0 and tm % nf == 0 and tm % DMA_GROUP == 0
    assert not emit_bf16 or cap == tm
    out_shape = [jax.ShapeDtypeStruct((n, d), F32)]
    out_specs = [pl.BlockSpec(memory_space=pl.ANY)]
    if emit_bf16:
        out_shape += [jax.ShapeDtypeStruct((n_e, d, dexp), BF16)] * 2 + [jax.ShapeDtypeStruct((n_e, dexp, d), BF16)]
        out_specs += [pl.BlockSpec((1, d, tf), lambda e, m, s, idx: (e, 0, jnp.minimum(s, nf - 1)))] * 2
        out_specs += [pl.BlockSpec((1, dexp, tn), lambda e, m, s, idx: (e, 0, jnp.maximum(s - nf, 0)))]
    grid_spec = pltpu.PrefetchScalarGridSpec(
        num_scalar_prefetch=1,
        grid=(n_e, cap // tm, nf + nd),
        in_specs=[pl.BlockSpec(memory_space=pl.ANY),
                  pl.BlockSpec((1, tm, 1), lambda e, m, s, idx: (e, m, 0)),
                  pl.BlockSpec((1, 1, d, tf), lambda e, m, s, idx: (li, e, 0, jnp.minimum(s, nf - 1))),
                  pl.BlockSpec((1, 1, d, tf), lambda e, m, s, idx: (li, e, 0, jnp.minimum(s, nf - 1))),
                  pl.BlockSpec((1, 1, dexp, tn), lambda e, m, s, idx: (li, e, 0, jnp.maximum(s - nf, 0))),
                  pl.BlockSpec(memory_space=pl.ANY)],
        out_specs=out_specs,
        scratch_shapes=[pltpu.VMEM((nf, tm // nf, d), F32), pltpu.VMEM((nf // 2, 2 * tm // nf, d), F32),
                        pltpu.VMEM((tm, d), BF16),
                        pltpu.VMEM((nf, tm, tf), BF16), pltpu.SemaphoreType.DMA, pltpu.SemaphoreType.DMA],
    )
    return pl.pallas_call(
        functools.partial(_moe_kernel, tm=tm, nf=nf, tf=tf, nd=nd, tn=tn, emit_bf16=emit_bf16),
        out_shape=out_shape, grid_spec=grid_spec,
        input_output_aliases={6: 0},
        compiler_params=_params(("arbitrary", "arbitrary", "arbitrary")), name="moe_ffn",
    )(idx.reshape(-1), h, gate.reshape(n_e, cap, 1), wg, wu, wd, jnp.zeros((n, d), F32))


def _mods(ada, lo, hi):
    d = ada.shape[1] // 6
    return [ada[lo:hi, k * d:(k + 1) * d].reshape(hi - lo, 1, d) for k in range(6)]


def _ab_mixer(h, w_in, lam_p, subln, lb, onorm, lam_init, rope=None, ctx=None):
    bsz, t, d = h.shape
    proj = _mm([h.reshape(bsz * t, d)], w_in).reshape(bsz, t, -1)
    bw = B_HEADS * B_KEY
    col0 = {"q": 3 * A_WIDTH, "zf": 3 * A_WIDTH + bw, "zb": 3 * A_WIDTH + 2 * bw,
            "v": 3 * A_WIDTH + 3 * bw, "og": 3 * A_WIDTH + 4 * bw}
    extra = (lb[0:1], lb[1:2])
    if ctx is None:
        o_a, k_c, v_c = _attention(proj, lam_p, subln, lam_init)
        o_b, states = _scan("hgrn", proj, col0, onorm, heads=B_HEADS, dk=B_KEY, dv=B_VAL, extra=extra,
                            emit_state=True)
        return (o_a, o_b), (k_c, v_c, states)
    ck, cv, cs = ctx
    (o_a,) = _attention(proj, lam_p, subln, lam_init, ctx=(ck, cv), rope=rope)
    (o_b,) = _scan("hgrn", proj, col0, onorm, heads=B_HEADS, dk=B_KEY, dv=B_VAL, extra=extra, s0=cs)
    return (o_a, o_b), None


def _gla_mixer(h, w_main, w_lr, w2f, w2b, gbf, gbb, onorm, ctx_state=None):
    bsz, t, d = h.shape
    h2d = h.reshape(bsz * t, d)
    n_main = C_HEADS * (2 * C_KEY + 2 * C_VAL)
    proj = _mm([h2d], w_main, n_cols=n_main).reshape(bsz, t, n_main)
    lr = _mm([h2d], w_lr).reshape(bsz, t, LANES)
    col0 = {"q": 0, "k": C_HEADS * C_KEY, "v": 2 * C_HEADS * C_KEY, "og": 2 * C_HEADS * C_KEY + C_HEADS * C_VAL}
    extra = (lr, w2f, w2b, gbf, gbb)
    if ctx_state is None:
        o, states = _scan("gla", proj, col0, onorm, heads=C_HEADS, dk=C_KEY, dv=C_VAL, extra=extra,
                          emit_state=True)
        return (o,), states
    (o,) = _scan("gla", proj, col0, onorm, heads=C_HEADS, dk=C_KEY, dv=C_VAL, extra=extra, s0=ctx_state)
    return (o,), None


def kernel(x_prompt, x_sample, cache_k_a, cache_v_a, state_hgrn, state_gla, c, c_ctx, ada_w, ada_b, norm_mix, norm_ffn, ab_w_in, ab_lambda, ab_subln, ab_lb_logits, ab_onorm, ab_w_out, c_w_in, c_gate_w2, c_gate_b, c_onorm, c_w_out, router_w, exp_w_gate, exp_w_up, exp_w_down, final_norm):
    d = x_prompt.shape[-1]
    n_s = c.shape[0]
    lb_table = jnp.cumsum(jax.nn.softmax(ab_lb_logits.astype(F32), axis=0), axis=0)
    rope = _rope_tables(x_sample.shape[1])
    cond_rows = 8 * ((1 + n_s + 7) // 8)
    cond = jnp.concatenate([c_ctx[None, :], c, jnp.zeros((cond_rows - 1 - n_s, d), F32)], axis=0)

    ys = [x_prompt, x_sample]
    pending = [None, None]
    new_k = new_v = new_hgrn = new_gla = None
    for li in range(DEPTH):
        ada = _adaln(cond, ada_w, ada_b, li)
        mods = [_mods(ada, 0, 1), _mods(ada, 1, 1 + n_s)]
        j = li // 2
        if li % 2 == 0:
            lam_init = 0.8 - 0.6 * math.exp(-0.3 * li)
            w_in = ab_w_in[j].astype(BF16)
            w_out = ab_w_out[j].astype(BF16)
        else:
            n_main = C_HEADS * (2 * C_KEY + 2 * C_VAL)
            w_in = c_w_in[j][:, :n_main].astype(BF16)
            w_lr = jnp.pad(c_w_in[j][:, n_main:], ((0, 0), (0, LANES - 2 * C_RANK))).astype(BF16)
            w2 = c_gate_w2[j]
            w2f = jnp.pad(w2[0], ((0, LANES - C_RANK), (0, 0))).astype(BF16)
            w2b = jnp.pad(w2[1], ((C_RANK, LANES - 2 * C_RANK), (0, 0))).astype(BF16)
            gbf, gbb = c_gate_b[j][0:1], c_gate_b[j][1:2]
            w_out = c_w_out[j].astype(BF16)
        for gi in range(2):
            sh1, sc1, g1, sh2, sc2, g2 = mods[gi]
            y = ys[gi]
            bsz, t, _ = y.shape
            if pending[gi] is None:
                (h,) = _norm(y, norm_mix[li], mod=(sc1, sh1), out_dtype=BF16)
            else:
                y, h = _norm(y, norm_mix[li], resid=pending[gi], mod=(sc1, sh1), emit_y=True, out_dtype=BF16)
            is_ctx = gi == 0
            if li % 2 == 0:
                ctx = None if is_ctx else (cache_k_a[:, j], cache_v_a[:, j], state_hgrn[:, j])
                outs, cache = _ab_mixer(h, w_in, ab_lambda[j], ab_subln[j], lb_table[li], ab_onorm[j], lam_init,
                                        rope=rope, ctx=ctx)
                if is_ctx:
                    new_k, new_v, new_hgrn = cache
            else:
                outs, cache = _gla_mixer(h, w_in, w_lr, w2f, w2b, gbf, gbb, c_onorm[j],
                                         ctx_state=None if is_ctx else state_gla[:, j])
                if is_ctx:
                    new_gla = cache
            xs = [o.reshape(bsz * t, -1) for o in outs]
            y1 = _mm(xs, w_out, resid=(y.reshape(bsz * t, d), g1), rows_per_cond=t).reshape(bsz, t, d)
            h2, aff = _norm(y1, norm_ffn[li], mod=(sc2, sh2), router_w=router_w[li], out_dtype=F32)
            n_tok = bsz * t
            n_e = router_w.shape[-1]
            cap = EC_FACTOR * n_tok // n_e
            gate, idx = lax.top_k(aff.reshape(n_tok, ROUTER_PAD)[:, :n_e].T, cap)
            if gi == 0:
                moe, wg, wu, wd = _moe(h2.reshape(n_tok, d), idx, gate, exp_w_gate, exp_w_up, exp_w_down, li,
                                       emit_bf16=True)
            else:
                (moe,) = _moe(h2.reshape(n_tok, d), idx, gate, wg[None], wu[None], wd[None], 0)
            moe = moe.reshape(bsz, t, d)
            ys[gi] = y1
            pending[gi] = (moe, g2)
    outs = []
    for gi in range(2):
        (yo,) = _norm(ys[gi], final_norm, resid=pending[gi], out_dtype=F32)
        outs.append(yo)
    bp, tp = x_prompt.shape[:2]
    return (outs[0], outs[1],
            new_k.reshape(bp, tp, A_HEADS, 2 * A_QK)[:, None],
            new_v.reshape(bp, tp, A_HEADS, A_V)[:, None],
            new_hgrn[:, None], new_gla[:, None])
```

```python
import functools
import math

import numpy as np
import jax
import jax.numpy as jnp
from jax import lax
from jax.experimental import pallas as pl
from jax.experimental.pallas import tpu as pltpu

D_MODEL = 4096
DEPTH = 2
GRID_W = 64
A_HEADS = 8
A_V = 256
A_QK = 128
A_WIDTH = A_HEADS * A_V
B_KEY = 128
B_VAL = 128
B_WIDTH = D_MODEL // 2
B_HEADS = B_WIDTH // B_KEY
C_HEADS = 8
C_KEY = D_MODEL // 2 // C_HEADS
C_VAL = D_MODEL // C_HEADS
C_RANK = 16
C_GATE_NORM = 16.0
N_EXPERTS = 16
EC_FACTOR = 2
D_EXPERT = D_MODEL // 2
CHUNK = 64
ROPE_BASE = 10000.0
EPS = 1e-6
LOG2E = math.log2(math.e)
F32 = jnp.float32
BF16 = jnp.bfloat16

LANES = 128
MXU_DIM = 256
VMEM_LIMIT = 56 * 1024 * 1024
ROUTER_PAD = LANES


def _params(sem, vmem=VMEM_LIMIT):
    return pltpu.CompilerParams(dimension_semantics=sem, vmem_limit_bytes=vmem)


def _adaln_kernel(c_ref, w_ref, b_ref, o_ref):
    c = c_ref[...]
    a = (c * jax.nn.sigmoid(c)).astype(BF16)
    o_ref[...] = jnp.dot(a, w_ref[...].astype(BF16), preferred_element_type=F32) + b_ref[...]


def _adaln(cond, w, b, li):
    rows, d = cond.shape
    n_layers, _, n = w.shape
    tn = 512
    return pl.pallas_call(
        _adaln_kernel,
        out_shape=jax.ShapeDtypeStruct((rows, n), F32),
        grid=(n // tn,),
        in_specs=[pl.BlockSpec((rows, d), lambda j: (0, 0)),
                  pl.BlockSpec((None, d, tn), lambda j: (li, 0, j)),
                  pl.BlockSpec((None, 1, tn), lambda j: (li, 0, j))],
        out_specs=pl.BlockSpec((rows, tn), lambda j: (0, j)),
        compiler_params=_params(("arbitrary",)),
        name="adaln",
    )(cond, w, b.reshape(n_layers, 1, n))


def _norm_kernel(*refs, has_resid, has_mod, has_router, emit_y, n_experts):
    it = iter(refs)
    y_ref = next(it)
    if has_resid:
        m_ref, gate_ref = next(it), next(it)
    g_ref = next(it)
    if has_mod:
        sc_ref, sh_ref = next(it), next(it)
    if has_router:
        rwh_ref, rwl_ref = next(it), next(it)
    if emit_y:
        yo_ref = next(it)
    h_ref = next(it)
    if has_router:
        aff_ref = next(it)

    x = y_ref[0]
    if has_resid:
        x = x + gate_ref[0] * m_ref[0]
    if emit_y:
        yo_ref[0] = x
    ms = jnp.mean(x * x, axis=-1, keepdims=True)
    h = (x * lax.rsqrt(ms + EPS)) * g_ref[...]
    if has_mod:
        h = h * (1.0 + sc_ref[0]) + sh_ref[0]
    h_ref[0] = h.astype(h_ref.dtype)
    if has_router:
        h_hi = h.astype(BF16)
        h_lo = (h - h_hi.astype(F32)).astype(BF16)
        logits = (jnp.dot(h_hi, rwh_ref[...], preferred_element_type=F32)
                  + (jnp.dot(h_hi, rwl_ref[...], preferred_element_type=F32)
                     + jnp.dot(h_lo, rwh_ref[...], preferred_element_type=F32)))
        col = lax.broadcasted_iota(jnp.int32, logits.shape, 1)
        logits = jnp.where(col < n_experts, logits, -jnp.inf)
        e = jnp.exp(logits - jnp.max(logits, axis=-1, keepdims=True))
        aff_ref[0] = e / jnp.sum(e, axis=-1, keepdims=True)


def _norm(y, gnorm, *, resid=None, mod=None, router_w=None, emit_y=False, out_dtype=F32, tt=256):
    bsz, t, d = y.shape
    tt = min(tt, t)
    tok = pl.BlockSpec((1, tt, d), lambda b, i: (b, i, 0))

    def cond_spec(a):
        if a.shape[0] == 1:
            return pl.BlockSpec((1, 1, d), lambda b, i: (0, 0, 0))
        return pl.BlockSpec((1, 1, d), lambda b, i: (b, 0, 0))

    args, specs = [y], [tok]
    if resid is not None:
        args += [resid[0], resid[1]]
        specs += [tok, cond_spec(resid[1])]
    args.append(gnorm.reshape(1, d))
    specs.append(pl.BlockSpec((1, d), lambda b, i: (0, 0)))
    if mod is not None:
        args += [mod[0], mod[1]]
        specs += [cond_spec(mod[0]), cond_spec(mod[1])]
    if router_w is not None:
        rw = jnp.pad(router_w, ((0, 0), (0, ROUTER_PAD - router_w.shape[1])))
        rw_hi = rw.astype(BF16)
        args += [rw_hi, (rw - rw_hi.astype(F32)).astype(BF16)]
        specs += [pl.BlockSpec((d, ROUTER_PAD), lambda b, i: (0, 0))] * 2
    out_shape, out_specs = [], []
    if emit_y:
        out_shape.append(jax.ShapeDtypeStruct((bsz, t, d), F32))
        out_specs.append(tok)
    out_shape.append(jax.ShapeDtypeStruct((bsz, t, d), out_dtype))
    out_specs.append(tok)
    if router_w is not None:
        out_shape.append(jax.ShapeDtypeStruct((bsz, t, ROUTER_PAD), F32))
        out_specs.append(pl.BlockSpec((1, tt, ROUTER_PAD), lambda b, i: (b, i, 0)))
    kern = functools.partial(_norm_kernel, has_resid=resid is not None, has_mod=mod is not None,
                             has_router=router_w is not None, emit_y=emit_y,
                             n_experts=0 if router_w is None else router_w.shape[1])
    return pl.pallas_call(
        kern, out_shape=out_shape, grid=(bsz, t // tt), in_specs=specs, out_specs=out_specs,
        compiler_params=_params(("arbitrary", "arbitrary")), name="norm",
    )(*args)


def _mm_kernel(*refs, n_in, has_resid):
    xs = refs[:n_in]
    ws = refs[n_in:2 * n_in]
    rest = refs[2 * n_in:]
    acc = jnp.dot(xs[0][...], ws[0][...], preferred_element_type=F32)
    for x_ref, w_ref in zip(xs[1:], ws[1:]):
        acc = acc + jnp.dot(x_ref[...], w_ref[...], preferred_element_type=F32)
    if has_resid:
        y_ref, gate_ref, o_ref = rest
        o_ref[...] = y_ref[...] + gate_ref[0] * acc
    else:
        (o_ref,) = rest
        o_ref[...] = acc.astype(o_ref.dtype)


def _mm(xs, w, *, n_cols=None, resid=None, rows_per_cond=None, out_dtype=F32, tm=1024, tn=1024):
    m = xs[0].shape[0]
    n = w.shape[1] if n_cols is None else n_cols
    tm, tn = min(tm, m), min(tn, n)
    if resid is not None and resid[1].shape[0] > 1:
        tm = min(tm, rows_per_cond)
    specs, args = [], []
    for x in xs:
        args.append(x)
        specs.append(pl.BlockSpec((tm, x.shape[1]), lambda i, j: (i, 0)))
    kb = 0
    for x in xs:
        k = x.shape[1]
        args.append(w)
        specs.append(pl.BlockSpec((k, tn), functools.partial(lambda i, j, r: (r, j), r=kb // k)))
        kb += k
    if resid is not None:
        y, gate = resid
        args += [y, gate]
        specs.append(pl.BlockSpec((tm, tn), lambda i, j: (i, j)))
        if gate.shape[0] == 1:
            specs.append(pl.BlockSpec((1, 1, tn), lambda i, j: (0, 0, j)))
        else:
            per = rows_per_cond // tm
            specs.append(pl.BlockSpec((1, 1, tn), lambda i, j: (i // per, 0, j)))
    kern = functools.partial(_mm_kernel, n_in=len(xs), has_resid=resid is not None)
    return pl.pallas_call(
        kern, out_shape=jax.ShapeDtypeStruct((m, n), out_dtype), grid=(m // tm, n // tn),
        in_specs=specs, out_specs=pl.BlockSpec((tm, tn), lambda i, j: (i, j)),
        compiler_params=_params(("arbitrary", "arbitrary")), name="matmul",
    )(*args)


def _rope_tables(n_tokens):
    rows = n_tokens // GRID_W
    row = jnp.repeat(jnp.arange(rows, dtype=F32), GRID_W)
    col = jnp.tile(jnp.arange(GRID_W, dtype=F32), rows)
    axis_dim = A_QK // 2
    inv = ROPE_BASE ** (-jnp.arange(0, axis_dim, 2, dtype=F32) / axis_dim)
    ar, ac = row[:, None] * inv, col[:, None] * inv
    cos = jnp.concatenate([jnp.cos(ar), jnp.cos(ar), jnp.cos(ac), jnp.cos(ac)], axis=-1)
    sin = jnp.concatenate([-jnp.sin(ar), jnp.sin(ar), -jnp.sin(ac), jnp.sin(ac)], axis=-1)
    return cos, sin


def _rope(x, cos, sin):
    q = A_QK // 4
    lane = lax.broadcasted_iota(jnp.int32, x.shape, 1)
    first = (lane % (2 * q)) < q
    partner = jnp.where(first, pltpu.roll(x, A_QK - q, axis=1), pltpu.roll(x, q, axis=1))
    return x * cos + partner * sin


def _attn_kernel(*refs, use_ctx, emit_kv, lam_init, t_own):
    it = iter(refs)
    q_ref, k_ref, v_ref = next(it), next(it), next(it)
    if use_ctx:
        ck_ref, cv_ref, cq_ref, sq_ref, ckk_ref, skk_ref = (next(it) for _ in range(6))
    lam_ref, sub_ref = next(it), next(it)
    o_ref = next(it)
    if emit_kv:
        nk_ref, nv_ref = next(it), next(it)
    kb_ref, vb_ref = next(it), next(it)

    @pl.when(pl.program_id(2) == 0)
    def _():
        k = k_ref[0]
        v = v_ref[0]
        if emit_kv:
            nk_ref[0] = k
            nv_ref[0] = v
        for mp in range(2):
            km = k[:, mp * A_QK:(mp + 1) * A_QK]
            if use_ctx:
                km = _rope(km, ckk_ref[...], skk_ref[...])
            kb_ref[0:t_own, mp * A_QK:(mp + 1) * A_QK] = km.astype(BF16)
        vb_ref[0:t_own, :] = v.astype(BF16)
        if use_ctx:
            kb_ref[t_own:, :] = ck_ref[0].astype(BF16)
            vb_ref[t_own:, :] = cv_ref[0].astype(BF16)

    lp = lam_ref[...]
    lam = (jnp.exp(jnp.sum(lp[0:1] * lp[1:2], axis=-1, keepdims=True))
           - jnp.exp(jnp.sum(lp[2:3] * lp[3:4], axis=-1, keepdims=True)) + lam_init)
    q = q_ref[0]
    scale = (A_QK ** -0.5) * LOG2E
    es, ls = [], []
    for mp in range(2):
        qm = q[:, mp * A_QK:(mp + 1) * A_QK]
        if use_ctx:
            qm = _rope(qm, cq_ref[...], sq_ref[...])
        qm = (qm * scale).astype(BF16)
        s = lax.dot_general(qm, kb_ref[:, mp * A_QK:(mp + 1) * A_QK], (((1,), (1,)), ((), ())),
                            preferred_element_type=F32)
        e = jnp.exp2(s - jnp.max(s, axis=-1, keepdims=True))
        es.append(e)
        ls.append(jnp.sum(e, axis=-1, keepdims=True))
    w = es[0] * (1.0 / ls[0]) - es[1] * (lam / ls[1])
    o = jnp.dot(w.astype(BF16), vb_ref[...], preferred_element_type=F32)
    o = o * lax.rsqrt(jnp.mean(o * o, axis=-1, keepdims=True) + EPS) * sub_ref[...]
    o_ref[0] = (o * (1.0 - lam_init)).astype(o_ref.dtype)


def _attention(proj, lam_p, subln, lam_init, ctx=None, rope=None, tq=512):
    bsz, t, _ = proj.shape
    tq = min(tq, t)
    hw = 2 * A_QK
    use_ctx = ctx is not None
    args = [proj, proj, proj]
    specs = [pl.BlockSpec((1, tq, hw), lambda b, h, i: (b, i, h)),
             pl.BlockSpec((1, t, hw), lambda b, h, i: (b, 0, A_HEADS + h)),
             pl.BlockSpec((1, t, A_V), lambda b, h, i: (b, 0, 2 * A_HEADS + h))]
    t_all = t
    if use_ctx:
        ck, cv = ctx
        p = ck.shape[1]
        t_all = t + p
        cos, sin = rope
        args += [ck.reshape(bsz, p, A_HEADS * hw), cv.reshape(bsz, p, A_WIDTH), cos, sin, cos, sin]
        specs += [pl.BlockSpec((1, p, hw), lambda b, h, i: (b, 0, h)),
                  pl.BlockSpec((1, p, A_V), lambda b, h, i: (b, 0, h)),
                  pl.BlockSpec((tq, A_QK), lambda b, h, i: (i, 0)),
                  pl.BlockSpec((tq, A_QK), lambda b, h, i: (i, 0)),
                  pl.BlockSpec((t, A_QK), lambda b, h, i: (0, 0)),
                  pl.BlockSpec((t, A_QK), lambda b, h, i: (0, 0))]
    args += [lam_p, subln.reshape(1, A_V)]
    specs += [pl.BlockSpec((4, A_QK), lambda b, h, i: (0, 0)),
              pl.BlockSpec((1, A_V), lambda b, h, i: (0, 0))]
    out_shape = [jax.ShapeDtypeStruct((bsz, t, A_WIDTH), BF16)]
    out_specs = [pl.BlockSpec((1, tq, A_V), lambda b, h, i: (b, i, h))]
    if not use_ctx:
        out_shape += [jax.ShapeDtypeStruct((bsz, t, A_WIDTH), F32)] * 2
        out_specs += [pl.BlockSpec((1, t, hw), lambda b, h, i: (b, 0, h)),
                      pl.BlockSpec((1, t, A_V), lambda b, h, i: (b, 0, h))]
    kern = functools.partial(_attn_kernel, use_ctx=use_ctx, emit_kv=not use_ctx, lam_init=lam_init, t_own=t)
    return pl.pallas_call(
        kern, out_shape=out_shape, grid=(bsz, A_HEADS, t // tq), in_specs=specs, out_specs=out_specs,
        scratch_shapes=[pltpu.VMEM((t_all, hw), BF16), pltpu.VMEM((t_all, A_V), BF16)],
        compiler_params=_params(("arbitrary", "arbitrary", "arbitrary")), name="diff_attention",
    )(*args)


def _level_table(rev):
    i = np.arange(CHUNK)[:, None]
    j = np.arange(CHUNK)[None, :]
    x = i ^ j
    lvl = np.where(x == 0, 0, 2 ** np.floor(np.log2(np.maximum(x, 1))).astype(np.int64))
    keep = (i <= j) if rev else (i >= j)
    return jnp.asarray(np.where(keep, lvl, -1), dtype=jnp.int32)


def _chunk_local(q, k, g, lvl, rev):
    c, dk = g.shape
    row = lax.broadcasted_iota(jnp.int32, (c, dk), 0)
    pos = (c - 1 - row) if rev else row

    def prev(x, d):
        return pltpu.roll(x, (c - d) if rev else d, axis=0)

    sub = 8
    ng = c // sub
    p8 = pos % sub

    pre = g
    for d in (1, 2, 4):
        pre = pre + jnp.where(p8 >= d, prev(pre, d), 0.0)
    groups = [pre[sub * i:sub * (i + 1)] for i in range(ng)]
    tot_row = 0 if rev else sub - 1
    off = None
    b_groups = [None] * ng
    for i in (reversed(range(ng)) if rev else range(ng)):
        b_groups[i] = groups[i] if off is None else groups[i] + off
        tot = groups[i][tot_row:tot_row + 1, :]
        off = tot if off is None else off + tot
    b = jnp.concatenate(b_groups, axis=0)

    def group_row(p):
        a = (sub - 1 - p) if rev else p
        return jnp.concatenate([jnp.broadcast_to(gr[a:a + 1, :], (sub, dk)) for gr in groups], axis=0)

    ref2 = jnp.where(p8 < 4, group_row(1), group_row(5))
    ref4 = group_row(3)
    args = {
        1: jnp.where(p8 % 2 == 1, g, 0.0),
        2: jnp.where(p8 % 4 >= 2, pre - ref2, ref2 - pre),
        4: jnp.where(p8 >= 4, pre - ref4, ref4 - pre),
    }
    s = 8
    while s < c:
        parts = []
        for blk in range(c // (2 * s)):
            m = blk * 2 * s + (s if rev else s - 1)
            parts.append(jnp.broadcast_to(b[m:m + 1, :], (2 * s, dk)))
        bref = jnp.concatenate(parts, axis=0) if len(parts) > 1 else parts[0]
        args[s] = jnp.where(pos % (2 * s) >= s, b - bref, bref - b)
        s *= 2

    nt = (((1,), (1,)), ((), ()))
    a = jnp.where(lvl == 0, lax.dot_general(q.astype(BF16), k.astype(BF16), nt, preferred_element_type=F32), 0.0)
    for s, arg in args.items():
        e = jnp.exp2(arg)
        a_s = lax.dot_general((q * e).astype(BF16), (k * e).astype(BF16), nt, preferred_element_type=F32)
        a = jnp.where(lvl == s, a_s, a)

    last = 0 if rev else c - 1
    b_last = b[last:last + 1, :]
    q_dec = (q * jnp.exp2(b)).astype(BF16)
    k_dec = (k * jnp.exp2(b_last - b)).astype(BF16)
    return a.astype(BF16), q_dec, k_dec, jnp.exp2(b_last)


def _chunk_state(local, v, st):
    a, q_dec, k_dec, decay = local
    vb = v.astype(BF16)
    o = lax.dot_general(q_dec, st.astype(BF16), (((1,), (1,)), ((), ())), preferred_element_type=F32)
    o = o + jnp.dot(a, vb, preferred_element_type=F32)
    st_new = decay * st + lax.dot_general(vb, k_dec, (((0,), (0,)), ((), ())), preferred_element_type=F32)
    return o, st_new


def _scan_kernel(*refs, mode, has_s0, emit_state, t, dk, dv):
    it = iter(refs)
    q_ref = next(it)
    if mode == "hgrn":
        zf_ref, zb_ref, v_ref, og_ref, lbf_ref, lbb_ref = (next(it) for _ in range(6))
    else:
        k_ref, v_ref, og_ref, lr_ref, w2f_ref, w2b_ref, gbf_ref, gbb_ref = (next(it) for _ in range(8))
    if has_s0:
        s0f_ref, s0b_ref = next(it), next(it)
    onorm_ref, lvf_ref, lvb_ref = next(it), next(it), next(it)
    o_ref = next(it)
    if emit_state:
        sout_ref = next(it)
    oacc_ref, stf_ref, stb_ref = next(it), next(it), next(it)
    n = t // CHUNK
    half = n // 2

    def chunk_rows(step, rev):
        ci = (n - 1 - step) if rev else step
        return pl.ds(pl.multiple_of(ci * CHUNK, CHUNK), CHUNK)

    def local(step, rev):
        rows = chunk_rows(step, rev)
        q = q_ref[0, rows, :]
        if mode == "hgrn":
            z = (zb_ref if rev else zf_ref)[0, rows, :]
            lb = (lbb_ref if rev else lbf_ref)[...]
            f = lb + (1.0 - lb) * jax.nn.sigmoid(z)
            k = 1.0 - f
            g = jnp.log2(f)
        else:
            k = k_ref[0, rows, :]
            q = q * (dk ** -0.5)
            x = jnp.dot(lr_ref[0, rows, :].astype(BF16), (w2b_ref if rev else w2f_ref)[...],
                        preferred_element_type=F32) + (gbb_ref if rev else gbf_ref)[...]
            g = jax.nn.log_sigmoid(x) * (LOG2E / C_GATE_NORM)
        return _chunk_local(q, k, g, (lvb_ref if rev else lvf_ref)[...], rev)

    if has_s0:
        stf_ref[...] = s0f_ref[0, 0, 0].T
        stb_ref[...] = s0b_ref[0, 0, 0].T
    else:
        stf_ref[...] = jnp.zeros_like(stf_ref)
        stb_ref[...] = jnp.zeros_like(stb_ref)

    unroll = 4 if half % 4 == 0 else (2 if half % 2 == 0 else 1)
    trips, half_trips = n // unroll, half // unroll
    lanes = [(u, rev) for u in range(unroll) for rev in (False, True)]

    def local_stage(trip):
        return tuple(local(trip * unroll + u, rev) for u, rev in lanes)

    def state_stage(trip, locals_, final):
        for (u, rev), loc in zip(lanes, locals_):
            rows = chunk_rows(trip * unroll + u, rev)
            st_ref = stb_ref if rev else stf_ref
            o, st_new = _chunk_state(loc, v_ref[0, rows, :], st_ref[...])
            st_ref[...] = st_new
            if not final:
                oacc_ref[rows, :] = o
            else:
                o = o + oacc_ref[rows, :]
                o = o * lax.rsqrt(jnp.mean(o * o, axis=-1, keepdims=True) + EPS) * onorm_ref[...]
                og = og_ref[0, rows, :]
                o_ref[0, rows, :] = (o * (og * jax.nn.sigmoid(og))).astype(o_ref.dtype)

    def body(trip, carry, final):
        nxt = local_stage(trip + 1)
        state_stage(trip, carry, final)
        return nxt

    carry = local_stage(0)
    carry = lax.fori_loop(0, half_trips, functools.partial(body, final=False), carry)
    carry = lax.fori_loop(half_trips, trips - 1, functools.partial(body, final=True), carry)
    state_stage(trips - 1, carry, True)
    if emit_state:
        sout_ref[0, 0, 0] = stf_ref[...].T
        sout_ref[0, 1, 0] = stb_ref[...].T


def _scan(mode, proj, col0, onorm, *, heads, dk, dv, extra, s0=None, emit_state=False):
    bsz, t, _ = proj.shape

    def col(name, width):
        base = col0[name] // width
        return pl.BlockSpec((1, t, width), lambda b, h: (b, 0, base + h))

    args, specs = [proj], [col("q", dk)]
    if mode == "hgrn":
        lbf, lbb = extra
        args += [proj, proj, proj, proj, lbf, lbb]
        specs += [col("zf", dk), col("zb", dk), col("v", dv), col("og", dv),
                  pl.BlockSpec((1, dk), lambda b, h: (0, h)), pl.BlockSpec((1, dk), lambda b, h: (0, h))]
    else:
        lr, w2f, w2b, gbf, gbb = extra
        args += [proj, proj, proj, lr, w2f, w2b, gbf, gbb]
        specs += [col("k", dk), col("v", dv), col("og", dv),
                  pl.BlockSpec((1, t, LANES), lambda b, h: (b, 0, 0)),
                  pl.BlockSpec((LANES, dk), lambda b, h: (0, h)), pl.BlockSpec((LANES, dk), lambda b, h: (0, h)),
                  pl.BlockSpec((1, dk), lambda b, h: (0, h)), pl.BlockSpec((1, dk), lambda b, h: (0, h))]
    st_block = (1, 1, 1, dk, dv)
    if s0 is not None:
        args += [s0, s0]
        specs += [pl.BlockSpec(st_block, lambda b, h: (b, 0, h, 0, 0)),
                  pl.BlockSpec(st_block, lambda b, h: (b, 1, h, 0, 0))]
    args += [onorm.reshape(1, dv), _level_table(False), _level_table(True)]
    specs += [pl.BlockSpec((1, dv), lambda b, h: (0, 0)),
              pl.BlockSpec((CHUNK, CHUNK), lambda b, h: (0, 0)),
              pl.BlockSpec((CHUNK, CHUNK), lambda b, h: (0, 0))]
    out_shape = [jax.ShapeDtypeStruct((bsz, t, heads * dv), BF16)]
    out_specs = [pl.BlockSpec((1, t, dv), lambda b, h: (b, 0, h))]
    if emit_state:
        out_shape += [jax.ShapeDtypeStruct((bsz, 2, heads, dk, dv), F32)]
        out_specs += [pl.BlockSpec((1, 2, 1, dk, dv), lambda b, h: (b, 0, h, 0, 0))]
    kern = functools.partial(_scan_kernel, mode=mode, has_s0=s0 is not None, emit_state=emit_state,
                             t=t, dk=dk, dv=dv)
    return pl.pallas_call(
        kern, out_shape=out_shape, grid=(bsz, heads), in_specs=specs, out_specs=out_specs,
        scratch_shapes=[pltpu.VMEM((t, dv), F32), pltpu.VMEM((dv, dk), F32), pltpu.VMEM((dv, dk), F32)],
        compiler_params=_params(("arbitrary", "arbitrary")), name="gated_scan_" + mode,
    )(*args)


DMA_GROUP = 8


def _moe_kernel(idx_ref, h_hbm, gate_ref, wg_ref, wu_ref, wd_ref, zero_hbm, out_hbm, *rest,
                tm, nf, tf, nd, tn, emit_bf16):
    if emit_bf16:
        wgb_ref, wub_ref, wdb_ref = rest[:3]
        rest = rest[3:]
    xrow_ref, orow_ref, xb_ref, hid_ref, sem_x, sem_o = rest
    del zero_hbm
    s = pl.program_id(2)
    tile = pl.program_id(0) * pl.num_programs(1) + pl.program_id(1)
    n_tiles = pl.num_programs(0) * pl.num_programs(1)
    base = tile * tm
    next_base = jnp.minimum(base + tm, (n_tiles - 1) * tm)
    prev_base = jnp.maximum(base - tm, 0)
    half = nf // 2
    rows_x, rows_o = tm // nf, tm // half

    def x_copy(b0, grp, u):
        tok = idx_ref[b0 + grp * rows_x + u]
        return pltpu.make_async_copy(h_hbm.at[pl.ds(tok, 1)], xrow_ref.at[grp, pl.ds(u, 1)], sem_x)

    def o_gather(grp, u):
        tok = idx_ref[base + grp * rows_o + u]
        return pltpu.make_async_copy(out_hbm.at[pl.ds(tok, 1)], orow_ref.at[grp, pl.ds(u, 1)], sem_o)

    def o_scatter(b0, grp, u):
        tok = idx_ref[b0 + grp * rows_o + u]
        return pltpu.make_async_copy(orow_ref.at[grp, pl.ds(u, 1)], out_hbm.at[pl.ds(tok, 1)], sem_o)

    def for_rows(fn, rows):
        per = rows // DMA_GROUP
        def body(i, c):
            for u in range(DMA_GROUP):
                fn(i // per, (i % per) * DMA_GROUP + u)
            return c
        lax.fori_loop(0, tm // DMA_GROUP, body, 0)

    @pl.when(s == 0)
    def _():
        @pl.when(tile == 0)
        def _():
            orow_ref[...] = jnp.zeros_like(orow_ref)
            for_rows(lambda g, u: x_copy(base, g, u).start(), rows_x)
        for_rows(lambda g, u: x_copy(base, g, u).wait(), rows_x)
        xb_ref[...] = xrow_ref[...].reshape(tm, xb_ref.shape[1]).astype(BF16)

    def up_step(first_half):
        for u in range(rows_x):
            x_copy(next_base, s, u).start()
        for u in range(rows_o):
            if first_half:
                o_scatter(prev_base, s, u).start()
            else:
                o_gather(s - half, u).start()
        x = xb_ref[...]
        wg, wu = wg_ref[0, 0], wu_ref[0, 0]
        if emit_bf16:
            wg, wu = wg.astype(BF16), wu.astype(BF16)
            wgb_ref[0] = wg
            wub_ref[0] = wu
        if 2 * tf <= MXU_DIM:
            gu = jnp.dot(x, jnp.concatenate([wg, wu], axis=1), preferred_element_type=F32)
            g, u = gu[:, :tf], gu[:, tf:]
        else:
            g = jnp.dot(x, wg, preferred_element_type=F32)
            u = jnp.dot(x, wu, preferred_element_type=F32)
        hid_ref[s] = (g * jax.nn.sigmoid(g) * u).astype(BF16)

    @pl.when(s < half)
    def _():
        up_step(True)

    @pl.when(s == half)
    def _():
        for_rows(lambda g, u: o_scatter(prev_base, g, u).wait(), rows_o)

    @pl.when((s >= half) & (s < nf))
    def _():
        up_step(False)

    for n in range(nd):
        @pl.when(s == nf + n)
        def _(n=n):
            if n == 0:
                for_rows(lambda g, u: o_gather(g, u).wait(), rows_o)
            if emit_bf16:
                wdb_ref[0] = wd_ref[0, 0].astype(BF16)
                wd = wdb_ref.at[0]
            else:
                wd = wd_ref.at[0, 0]
            per = max(1, MXU_DIM // tf)
            acc = None
            for f in range(0, nf, per):
                lhs = hid_ref[f] if per == 1 else jnp.concatenate([hid_ref[f + i] for i in range(per)], axis=1)
                part = jnp.dot(lhs, wd[f * tf:(f + per) * tf, :], preferred_element_type=F32)
                acc = part if acc is None else acc + part
            orow_ref[:, :, n * tn:(n + 1) * tn] += (acc * gate_ref[0]).reshape(half, rows_o, tn)
            if n == nd - 1:
                @pl.when(tile == n_tiles - 1)
                def _():
                    for_rows(lambda g, u: o_scatter(base, g, u).start(), rows_o)
                    for_rows(lambda g, u: o_scatter(base, g, u).wait(), rows_o)
                    for_rows(lambda g, u: x_copy(base, g, u).wait(), rows_x)


def _moe(h, idx, gate, wg, wu, wd, li, emit_bf16=False, tm=512):
    n, d = h.shape
    n_e, cap = idx.shape
    dexp = wg.shape[3]
    tf, tn = (128, 512) if emit_bf16 else (256, 1024)
    tm, tf, tn = min(tm, cap), min(tf, dexp), min(tn, d)
    nf, nd = dexp // tf, d // tn
    assert nf % 2 == 0 and tm % nf == 0 and tm % DMA_GROUP == 0
    assert not emit_bf16 or cap == tm
    out_shape = [jax.ShapeDtypeStruct((n, d), F32)]
    out_specs = [pl.BlockSpec(memory_space=pl.ANY)]
    if emit_bf16:
        out_shape += [jax.ShapeDtypeStruct((n_e, d, dexp), BF16)] * 2 + [jax.ShapeDtypeStruct((n_e, dexp, d), BF16)]
        out_specs += [pl.BlockSpec((1, d, tf), lambda e, m, s, idx: (e, 0, jnp.minimum(s, nf - 1)))] * 2
        out_specs += [pl.BlockSpec((1, dexp, tn), lambda e, m, s, idx: (e, 0, jnp.maximum(s - nf, 0)))]
    grid_spec = pltpu.PrefetchScalarGridSpec(
        num_scalar_prefetch=1,
        grid=(n_e, cap // tm, nf + nd),
        in_specs=[pl.BlockSpec(memory_space=pl.ANY),
                  pl.BlockSpec((1, tm, 1), lambda e, m, s, idx: (e, m, 0)),
                  pl.BlockSpec((1, 1, d, tf), lambda e, m, s, idx: (li, e, 0, jnp.minimum(s, nf - 1))),
                  pl.BlockSpec((1, 1, d, tf), lambda e, m, s, idx: (li, e, 0, jnp.minimum(s, nf - 1))),
                  pl.BlockSpec((1, 1, dexp, tn), lambda e, m, s, idx: (li, e, 0, jnp.maximum(s - nf, 0))),
                  pl.BlockSpec(memory_space=pl.ANY)],
        out_specs=out_specs,
        scratch_shapes=[pltpu.VMEM((nf, tm // nf, d), F32), pltpu.VMEM((nf // 2, 2 * tm // nf, d), F32),
                        pltpu.VMEM((tm, d), BF16),
                        pltpu.VMEM((nf, tm, tf), BF16), pltpu.SemaphoreType.DMA, pltpu.SemaphoreType.DMA],
    )
    return pl.pallas_call(
        functools.partial(_moe_kernel, tm=tm, nf=nf, tf=tf, nd=nd, tn=tn, emit_bf16=emit_bf16),
        out_shape=out_shape, grid_spec=grid_spec,
        input_output_aliases={6: 0},
        compiler_params=_params(("arbitrary", "arbitrary", "arbitrary")), name="moe_ffn",
    )(idx.reshape(-1), h, gate.reshape(n_e, cap, 1), wg, wu, wd, jnp.zeros((n, d), F32))


def _mods(ada, lo, hi):
    d = ada.shape[1] // 6
    return [ada[lo:hi, k * d:(k + 1) * d].reshape(hi - lo, 1, d) for k in range(6)]


def _ab_mixer(h, w_in, lam_p, subln, lb, onorm, lam_init, rope=None, ctx=None):
    bsz, t, d = h.shape
    proj = _mm([h.reshape(bsz * t, d)], w_in).reshape(bsz, t, -1)
    bw = B_HEADS * B_KEY
    col0 = {"q": 3 * A_WIDTH, "zf": 3 * A_WIDTH + bw, "zb": 3 * A_WIDTH + 2 * bw,
            "v": 3 * A_WIDTH + 3 * bw, "og": 3 * A_WIDTH + 4 * bw}
    extra = (lb[0:1], lb[1:2])
    if ctx is None:
        o_a, k_c, v_c = _attention(proj, lam_p, subln, lam_init)
        o_b, states = _scan("hgrn", proj, col0, onorm, heads=B_HEADS, dk=B_KEY, dv=B_VAL, extra=extra,
                            emit_state=True)
        return (o_a, o_b), (k_c, v_c, states)
    ck, cv, cs = ctx
    (o_a,) = _attention(proj, lam_p, subln, lam_init, ctx=(ck, cv), rope=rope)
    (o_b,) = _scan("hgrn", proj, col0, onorm, heads=B_HEADS, dk=B_KEY, dv=B_VAL, extra=extra, s0=cs)
    return (o_a, o_b), None


def _gla_mixer(h, w_main, w_lr, w2f, w2b, gbf, gbb, onorm, ctx_state=None):
    bsz, t, d = h.shape
    h2d = h.reshape(bsz * t, d)
    n_main = C_HEADS * (2 * C_KEY + 2 * C_VAL)
    proj = _mm([h2d], w_main, n_cols=n_main).reshape(bsz, t, n_main)
    lr = _mm([h2d], w_lr).reshape(bsz, t, LANES)
    col0 = {"q": 0, "k": C_HEADS * C_KEY, "v": 2 * C_HEADS * C_KEY, "og": 2 * C_HEADS * C_KEY + C_HEADS * C_VAL}
    extra = (lr, w2f, w2b, gbf, gbb)
    if ctx_state is None:
        o, states = _scan("gla", proj, col0, onorm, heads=C_HEADS, dk=C_KEY, dv=C_VAL, extra=extra,
                          emit_state=True)
        return (o,), states
    (o,) = _scan("gla", proj, col0, onorm, heads=C_HEADS, dk=C_KEY, dv=C_VAL, extra=extra, s0=ctx_state)
    return (o,), None


def kernel(x_prompt, x_sample, cache_k_a, cache_v_a, state_hgrn, state_gla, c, c_ctx, ada_w, ada_b, norm_mix, norm_ffn, ab_w_in, ab_lambda, ab_subln, ab_lb_logits, ab_onorm, ab_w_out, c_w_in, c_gate_w2, c_gate_b, c_onorm, c_w_out, router_w, exp_w_gate, exp_w_up, exp_w_down, final_norm):
    d = x_prompt.shape[-1]
    n_s = c.shape[0]
    lb_table = jnp.cumsum(jax.nn.softmax(ab_lb_logits.astype(F32), axis=0), axis=0)
    rope = _rope_tables(x_sample.shape[1])
    cond_rows = 8 * ((1 + n_s + 7) // 8)
    cond = jnp.concatenate([c_ctx[None, :], c, jnp.zeros((cond_rows - 1 - n_s, d), F32)], axis=0)

    ys = [x_prompt, x_sample]
    pending = [None, None]
    new_k = new_v = new_hgrn = new_gla = None
    for li in range(DEPTH):
        ada = _adaln(cond, ada_w, ada_b, li)
        mods = [_mods(ada, 0, 1), _mods(ada, 1, 1 + n_s)]
        j = li // 2
        if li % 2 == 0:
            lam_init = 0.8 - 0.6 * math.exp(-0.3 * li)
            w_in = ab_w_in[j].astype(BF16)
            w_out = ab_w_out[j].astype(BF16)
        else:
            n_main = C_HEADS * (2 * C_KEY + 2 * C_VAL)
            w_in = c_w_in[j][:, :n_main].astype(BF16)
            w_lr = jnp.pad(c_w_in[j][:, n_main:], ((0, 0), (0, LANES - 2 * C_RANK))).astype(BF16)
            w2 = c_gate_w2[j]
            w2f = jnp.pad(w2[0], ((0, LANES - C_RANK), (0, 0))).astype(BF16)
            w2b = jnp.pad(w2[1], ((C_RANK, LANES - 2 * C_RANK), (0, 0))).astype(BF16)
            gbf, gbb = c_gate_b[j][0:1], c_gate_b[j][1:2]
            w_out = c_w_out[j].astype(BF16)
        for gi in range(2):
            sh1, sc1, g1, sh2, sc2, g2 = mods[gi]
            y = ys[gi]
            bsz, t, _ = y.shape
            if pending[gi] is None:
                (h,) = _norm(y, norm_mix[li], mod=(sc1, sh1), out_dtype=BF16)
            else:
                y, h = _norm(y, norm_mix[li], resid=pending[gi], mod=(sc1, sh1), emit_y=True, out_dtype=BF16)
            is_ctx = gi == 0
            if li % 2 == 0:
                ctx = None if is_ctx else (cache_k_a[:, j], cache_v_a[:, j], state_hgrn[:, j])
                outs, cache = _ab_mixer(h, w_in, ab_lambda[j], ab_subln[j], lb_table[li], ab_onorm[j], lam_init,
                                        rope=rope, ctx=ctx)
                if is_ctx:
                    new_k, new_v, new_hgrn = cache
            else:
                outs, cache = _gla_mixer(h, w_in, w_lr, w2f, w2b, gbf, gbb, c_onorm[j],
                                         ctx_state=None if is_ctx else state_gla[:, j])
                if is_ctx:
                    new_gla = cache
            xs = [o.reshape(bsz * t, -1) for o in outs]
            y1 = _mm(xs, w_out, resid=(y.reshape(bsz * t, d), g1), rows_per_cond=t).reshape(bsz, t, d)
            h2, aff = _norm(y1, norm_ffn[li], mod=(sc2, sh2), router_w=router_w[li], out_dtype=F32)
            n_tok = bsz * t
            n_e = router_w.shape[-1]
            cap = EC_FACTOR * n_tok // n_e
            gate, idx = lax.top_k(aff.reshape(n_tok, ROUTER_PAD)[:, :n_e].T, cap)
            if gi == 0:
                moe, wg, wu, wd = _moe(h2.reshape(n_tok, d), idx, gate, exp_w_gate, exp_w_up, exp_w_down, li,
                                       emit_bf16=True)
            else:
                (moe,) = _moe(h2.reshape(n_tok, d), idx, gate, wg[None], wu[None], wd[None], 0)
            moe = moe.reshape(bsz, t, d)
            ys[gi] = y1
            pending[gi] = (moe, g2)
    outs = []
    for gi in range(2):
        (yo,) = _norm(ys[gi], final_norm, resid=pending[gi], out_dtype=F32)
        outs.append(yo)
    bp, tp = x_prompt.shape[:2]
    return (outs[0], outs[1],
            new_k.reshape(bp, tp, A_HEADS, 2 * A_QK)[:, None],
            new_v.reshape(bp, tp, A_HEADS, A_V)[:, None],
            new_hgrn[:, None], new_gla[:, None])
```
